```python
import jax, jax.numpy as jnp
from jax import lax
import numpy as np

D_MODEL = 2048
BATCH = 4
SEQ = 2048
DEPTH = 4
DEC_BATCH = 128
DEC_SEQ = 4
PAST_LEN = 16384
PAGE_SIZE = 128

POOL_WINDOWS = (2, 4, 8, 16)
N_POOL_GROUPS = len(POOL_WINDOWS)
POOL_GROUP = D_MODEL // 16
POOL_WIDTH = N_POOL_GROUPS * POOL_GROUP
POOL_BUF = max(POOL_WINDOWS) - 1
CONV_WIDTH = D_MODEL // 2
CONV_K = 3
CONV_BUF = CONV_K - 1
CHUNK = 128
N_SGU_GROUPS = 4
SGU_GROUP = D_MODEL // 16
SGU_WIDTH = N_SGU_GROUPS * SGU_GROUP
N_BRANCH = 3
D_FF = 4 * D_MODEL
N_MOD = 6
EPS = 1e-6
OFF_POOL = 0
OFF_CONV = OFF_POOL + POOL_WIDTH
OFF_SGU = OFF_CONV + 3 * CONV_WIDTH
OFF_GATE = OFF_SGU + 2 * SGU_WIDTH
N_IN = OFF_GATE + N_BRANCH * D_MODEL

kernel_name = "hybrid_pool_conv_sgu_decoder_step"


def rmsnorm(x, g):
    xf = x.astype(jnp.float32)
    r = xf * lax.rsqrt(jnp.mean(xf * xf, axis=-1, keepdims=True) + EPS)
    return (r * g.astype(jnp.float32)).astype(x.dtype)


def pool_mix(p, prefix, pos, w_grp, scale):
    N, T, _ = p.shape
    full = jnp.concatenate([prefix, p], axis=1).astype(jnp.float32)
    cs = jnp.pad(jnp.cumsum(full, axis=1), ((0, 0), (1, 0), (0, 0)))
    hi = cs[:, POOL_BUF + 1:POOL_BUF + 1 + T]
    pf = p.astype(jnp.float32)
    outs = []
    for g, w in enumerate(POOL_WINDOWS):
        sl = slice(g * POOL_GROUP, (g + 1) * POOL_GROUP)
        lo = cs[:, POOL_BUF + 1 - w:POOL_BUF + 1 - w + T, sl]
        cnt = jnp.minimum(pos + 1, w).astype(jnp.float32)[None, :, None]
        outs.append((hi[..., sl] - lo) / cnt - pf[..., sl])
    d = jnp.stack(outs, axis=2)
    y = jnp.einsum('ntgc,gcd->ntgd', d, w_grp).reshape(N, T, POOL_WIDTH) * scale
    return y.astype(p.dtype)


def short_conv(z, prefix, w_conv):
    T = z.shape[1]
    full = jnp.concatenate([prefix, z], axis=1)
    return sum(w_conv[k] * full[:, k:k + T] for k in range(CONV_K))


def spatial_gate(v, w_s, b_s, L):
    N, T, _ = v.shape
    vf = v.reshape(N, T // L, L, N_SGU_GROUPS, SGU_GROUP)
    w = jnp.tril(w_s[:, :L, :L])
    out = jnp.einsum('gij,nkjgc->nkigc', w, vf) + b_s[:, :L].T[None, None, :, :, None]
    return out.reshape(N, T, SGU_WIDTH)


def layer(x, c, pool_prefix, conv_prefix, pos, L, norm1, norm2, w_ada, b_ada, w_in,
          w_pool_grp, pool_scale, w_conv, sgu_norm, w_sgu, b_sgu,
          w_br_pool, w_br_conv, w_br_sgu, w_out, w_ff1, w_ff2):
    N, T, D = x.shape
    mod = (jax.nn.silu(c) @ w_ada + b_ada).reshape(N, 1, N_MOD, D)
    sh1, sc1, g1, sh2, sc2, g2 = [mod[:, :, i] for i in range(N_MOD)]
    h = rmsnorm(x, norm1) * (1 + sc1) + sh1
    proj = h @ w_in
    p = proj[..., OFF_POOL:OFF_CONV]
    xc, bc, cc = jnp.split(proj[..., OFF_CONV:OFF_SGU], 3, axis=-1)
    u, v = jnp.split(proj[..., OFF_SGU:OFF_GATE], 2, axis=-1)
    gates = jax.nn.sigmoid(proj[..., OFF_GATE:]).reshape(N, T, N_BRANCH, D)
    pool_out = pool_mix(p, pool_prefix, pos, w_pool_grp, pool_scale)
    z = cc * xc
    conv_out = bc * short_conv(z, conv_prefix, w_conv)
    v = rmsnorm(jax.nn.gelu(v), sgu_norm)
    sgu_out = jax.nn.gelu(u) * spatial_gate(v, w_sgu, b_sgu, L)
    merged = (gates[:, :, 0] * (pool_out @ w_br_pool)
              + gates[:, :, 1] * (conv_out @ w_br_conv)
              + gates[:, :, 2] * (sgu_out @ w_br_sgu))
    x = x + g1 * (merged @ w_out)
    h2 = rmsnorm(x, norm2) * (1 + sc2) + sh2
    x = x + g2 * (jnp.square(jax.nn.relu(h2 @ w_ff1)) @ w_ff2)
    new_pool = jnp.concatenate([pool_prefix, p], axis=1)[:, -POOL_BUF:]
    new_conv = jnp.concatenate([conv_prefix, z], axis=1)[:, -CONV_BUF:]
    new_v = v[:, T - L:]
    return x, new_pool, new_conv, new_v


def setup_inputs(seed: int = 0) -> dict:
    key = jax.random.key(seed)
    ks = jax.random.split(key, 24)
    f32 = jnp.float32
    nrm = lambda k, shape, s: jax.random.normal(k, shape, f32) * s
    return {
        "x_prompt": nrm(ks[0], (BATCH, SEQ, D_MODEL), 1.0),
        "x_sample": nrm(ks[1], (DEC_BATCH, DEC_SEQ, D_MODEL), 1.0),
        "state_pool": nrm(ks[2], (DEPTH, DEC_BATCH, POOL_BUF, POOL_WIDTH), 1.0),
        "state_conv": nrm(ks[3], (DEPTH, DEC_BATCH, CONV_BUF, CONV_WIDTH), 1.0),
        "c_prompt": nrm(ks[4], (BATCH, D_MODEL), 1.0),
        "c_sample": nrm(ks[5], (DEC_BATCH, D_MODEL), 1.0),
        "norm1": 1.0 + nrm(ks[6], (DEPTH, D_MODEL), 0.05),
        "norm2": 1.0 + nrm(ks[7], (DEPTH, D_MODEL), 0.05),
        "w_ada": nrm(ks[8], (DEPTH, D_MODEL, N_MOD * D_MODEL), 0.5 * D_MODEL ** -0.5),
        "b_ada": nrm(ks[9], (DEPTH, N_MOD * D_MODEL), 0.02),
        "w_in": nrm(ks[10], (DEPTH, D_MODEL, N_IN), D_MODEL ** -0.5),
        "w_pool_grp": nrm(ks[11], (DEPTH, N_POOL_GROUPS, POOL_GROUP, POOL_GROUP), POOL_GROUP ** -0.5),
        "pool_scale": 1.0 + nrm(ks[12], (DEPTH, POOL_WIDTH), 0.1),
        "w_conv": nrm(ks[13], (DEPTH, CONV_K, CONV_WIDTH), CONV_K ** -0.5),
        "sgu_norm": 1.0 + nrm(ks[14], (DEPTH, SGU_WIDTH), 0.05),
        "w_sgu": nrm(ks[15], (DEPTH, N_SGU_GROUPS, CHUNK, CHUNK), CHUNK ** -0.5),
        "b_sgu": 1.0 + nrm(ks[16], (DEPTH, N_SGU_GROUPS, CHUNK), 0.1),
        "w_br_pool": nrm(ks[17], (DEPTH, POOL_WIDTH, D_MODEL), POOL_WIDTH ** -0.5),
        "w_br_conv": nrm(ks[18], (DEPTH, CONV_WIDTH, D_MODEL), CONV_WIDTH ** -0.5),
        "w_br_sgu": nrm(ks[19], (DEPTH, SGU_WIDTH, D_MODEL), SGU_WIDTH ** -0.5),
        "w_out": nrm(ks[20], (DEPTH, D_MODEL, D_MODEL), D_MODEL ** -0.5),
        "w_ff1": nrm(ks[21], (DEPTH, D_MODEL, D_FF), D_MODEL ** -0.5),
        "w_ff2": nrm(ks[22], (DEPTH, D_FF, D_MODEL), D_FF ** -0.5),
        "final_norm": 1.0 + nrm(ks[23], (D_MODEL,), 0.05),
    }


def reference(x_prompt, x_sample, state_pool, state_conv, c_prompt, c_sample,
              norm1, norm2, w_ada, b_ada, w_in, w_pool_grp, pool_scale, w_conv,
              sgu_norm, w_sgu, b_sgu, w_br_pool, w_br_conv, w_br_sgu, w_out,
              w_ff1, w_ff2, final_norm):
    Bp, Tp, _ = x_prompt.shape
    Ts = x_sample.shape[1]
    pos_p = jnp.arange(Tp, dtype=jnp.int32)
    pos_s = PAST_LEN + jnp.arange(Ts, dtype=jnp.int32)
    zero_pool = jnp.zeros((Bp, POOL_BUF, POOL_WIDTH), x_prompt.dtype)
    zero_conv = jnp.zeros((Bp, CONV_BUF, CONV_WIDTH), x_prompt.dtype)
    xp, xs = x_prompt, x_sample
    pp, cp, vp, ps, cs_, vs = [], [], [], [], [], []
    for l in range(DEPTH):
        params = (norm1[l], norm2[l], w_ada[l], b_ada[l], w_in[l], w_pool_grp[l],
                  pool_scale[l], w_conv[l], sgu_norm[l], w_sgu[l], b_sgu[l],
                  w_br_pool[l], w_br_conv[l], w_br_sgu[l], w_out[l], w_ff1[l], w_ff2[l])
        xp, a, b, c = layer(xp, c_prompt, zero_pool, zero_conv, pos_p, CHUNK, *params)
        pp.append(a); cp.append(b); vp.append(c)
        xs, a, b, c = layer(xs, c_sample, state_pool[l], state_conv[l], pos_s, Ts, *params)
        ps.append(a); cs_.append(b); vs.append(c)
    y_prompt = rmsnorm(xp, final_norm)
    y_sample = rmsnorm(xs, final_norm)
    return (y_prompt, y_sample, jnp.stack(pp), jnp.stack(cp), jnp.stack(vp),
            jnp.stack(ps), jnp.stack(cs_), jnp.stack(vs))
```

```python
import functools

import jax
import jax.numpy as jnp
from jax import lax
from jax.experimental import pallas as pl
from jax.experimental.pallas import tpu as pltpu

F32 = jnp.float32
BF16 = jnp.bfloat16

D_MODEL = 2048
DEPTH = 4
BATCH = 4
SEQ = 2048
DEC_BATCH = 128
DEC_SEQ = 4
PAST_LEN = 16384
POOL_WINDOWS = (2, 4, 8, 16)
GROUP = 128
POOL_WIDTH = 512
POOL_BUF = 15
CONV_WIDTH = 1024
CONV_K = 3
CONV_BUF = 2
CHUNK = 128
SGU_WIDTH = 512
N_MOD = 6
D_FF = 4 * D_MODEL
EPS = 1e-6
OFF_GATE = POOL_WIDTH + 3 * CONV_WIDTH + 2 * SGU_WIDTH
N_IN = OFF_GATE + 3 * D_MODEL

ROWS = 128
CB = 512
POOL_HALO = 16
CONV_HALO = 8
TM_PROMPT = 1024
M_PROMPT = BATCH * SEQ
M_SAMPLE = DEC_BATCH * DEC_SEQ
TN_GATE = 256
TN_OUT = 512
TF = 512
TN_ADA = 1024
ADA_ROWS = 136
VMEM_LIMIT = 60 * 1024 * 1024


def _dot(a, b):
    return jnp.dot(a, b, preferred_element_type=F32)


def _rms_scale(x, g):
    return x * lax.rsqrt(jnp.mean(x * x, axis=-1, keepdims=True) + EPS) * g


def _modulated_norm(x_ref, g_ref, sc_ref, sh_ref, h_ref, tm):
    def body(c, carry):
        r0 = pl.multiple_of(c * ROWS, ROWS)
        x = x_ref[pl.ds(r0, ROWS), :]
        h = _rms_scale(x, g_ref[...]) * (1.0 + sc_ref[...]) + sh_ref[...]
        h_ref[pl.ds(r0, ROWS), :] = h.astype(BF16)
        return carry
    lax.fori_loop(0, tm // ROWS, body, 0)


def _ada_kernel(c_ref, w_ref, b_ref, o_ref):
    a = jax.nn.silu(c_ref[...]).astype(BF16)
    o_ref[...] = _dot(a, w_ref[...].astype(BF16)) + b_ref[...]


def _ada_call(c_all, w_ada, b_ada):
    n = N_MOD * D_MODEL
    return pl.pallas_call(
        _ada_kernel,
        grid=(DEPTH, n // TN_ADA),
        in_specs=[
            pl.BlockSpec((ADA_ROWS, D_MODEL), lambda l, j: (0, 0)),
            pl.BlockSpec((None, D_MODEL, TN_ADA), lambda l, j: (l, 0, j)),
            pl.BlockSpec((None, 1, TN_ADA), lambda l, j: (l, 0, j)),
        ],
        out_specs=pl.BlockSpec((None, ADA_ROWS, TN_ADA), lambda l, j: (l, 0, j)),
        out_shape=jax.ShapeDtypeStruct((DEPTH, ADA_ROWS, n), F32),
        compiler_params=pltpu.CompilerParams(
            dimension_semantics=("arbitrary", "arbitrary"), vmem_limit_bytes=VMEM_LIMIT),
        name="ada",
    )(c_all, w_ada, b_ada.reshape(DEPTH, 1, n))


def _mix_prompt_kernel(x_ref, n1_ref, sh_ref, sc_ref, wa_ref, wb_ref, wc_ref, wgrp_ref, pscale_ref,
                       wconv_ref, sgun_ref, wsgu_ref, bsgu_ref,
                       br_ref, ptail_ref, ztail_ref, vtail_ref,
                       hbuf, t0, t1, pbuf, zbuf, zhalo, *, tm, tiles_per_seq):
    i = pl.program_id(0)
    j = pl.program_id(1)
    tile_in_seq = i % tiles_per_seq
    first_tile = tile_in_seq == 0
    n_slabs = tm // ROWS

    @pl.when(j == 0)
    def _():
        _modulated_norm(x_ref, n1_ref, sc_ref, sh_ref, hbuf, tm)

    @pl.when(j == 0)
    def _pool():
        @pl.when(first_tile)
        def _():
            pbuf[0:POOL_HALO, :] = jnp.zeros((POOL_HALO, CB), F32)

        @pl.when(jnp.logical_not(first_tile))
        def _():
            pbuf[0:POOL_HALO, :] = pbuf[tm:tm + POOL_HALO, :]

        pbuf[POOL_HALO:POOL_HALO + tm, :] = _dot(hbuf[...], wa_ref[...])
        ptail_ref[...] = pbuf[tm:tm + POOL_HALO, :]
        pos0 = tile_in_seq * tm

        def body(c, carry):
            r0 = pl.multiple_of(c * ROWS, ROWS)
            pos = pos0 + r0 + lax.broadcasted_iota(jnp.int32, (ROWS, 1), 0)
            for g, w in enumerate(POOL_WINDOWS):
                cols = slice(g * GROUP, (g + 1) * GROUP)
                ext = pbuf[pl.ds(r0, ROWS + POOL_HALO), cols]
                s = ext
                k = 1
                while k < w:
                    s = s + pltpu.roll(s, k, axis=0)
                    k *= 2
                cnt = jnp.minimum(pos + 1, w).astype(F32)
                d = s[POOL_HALO:, :] / cnt - ext[POOL_HALO:, :]
                y = _dot(d.astype(BF16), wgrp_ref[g].astype(BF16)) * pscale_ref[:, cols]
                br_ref[pl.ds(r0, ROWS), cols] = y.astype(BF16)
            return carry
        lax.fori_loop(0, n_slabs, body, 0)

    @pl.when(jnp.logical_and(j >= 1, j <= 2))
    def _conv():
        cb = j - 1

        @pl.when(first_tile)
        def _():
            zbuf[0:CONV_HALO, :] = jnp.zeros((CONV_HALO, CB), F32)

        @pl.when(jnp.logical_not(first_tile))
        def _():
            zbuf[0:CONV_HALO, :] = zhalo[cb]

        h = hbuf[...]
        zbuf[CONV_HALO:CONV_HALO + tm, :] = _dot(h, wc_ref[...]) * _dot(h, wa_ref[...])
        t0[...] = _dot(h, wb_ref[...])
        tail = zbuf[tm:tm + CONV_HALO, :]
        zhalo[cb] = tail
        ztail_ref[...] = tail
        w0 = wconv_ref[0:1, :]
        w1 = wconv_ref[1:2, :]
        w2 = wconv_ref[2:3, :]

        def body(c, carry):
            r0 = pl.multiple_of(c * ROWS, ROWS)
            ext = zbuf[pl.ds(r0, ROWS + CONV_HALO), :]
            conv = (w0 * pltpu.roll(ext, 2, axis=0)[CONV_HALO:, :]
                    + w1 * pltpu.roll(ext, 1, axis=0)[CONV_HALO:, :]
                    + w2 * ext[CONV_HALO:, :])
            br_ref[pl.ds(r0, ROWS), :] = (t0[pl.ds(r0, ROWS), :] * conv).astype(BF16)
            return carry
        lax.fori_loop(0, n_slabs, body, 0)

    @pl.when(j == 3)
    def _sgu():
        h = hbuf[...]
        t0[...] = _dot(h, wa_ref[...])
        t1[...] = _dot(h, wb_ref[...])
        row = lax.broadcasted_iota(jnp.int32, (CHUNK, CHUNK), 0)
        col = lax.broadcasted_iota(jnp.int32, (CHUNK, CHUNK), 1)
        causal = row >= col

        def body(c, carry):
            r0 = pl.multiple_of(c * ROWS, ROWS)
            vn = _rms_scale(jax.nn.gelu(t1[pl.ds(r0, ROWS), :]), sgun_ref[...])

            @pl.when(c == n_slabs - 1)
            def _():
                vtail_ref[...] = vn

            for g in range(SGU_WIDTH // GROUP):
                cols = slice(g * GROUP, (g + 1) * GROUP)
                wt = jnp.where(causal, wsgu_ref[g], 0.0).astype(BF16)
                sg = _dot(wt, vn[:, cols].astype(BF16)) + bsgu_ref[:, g:g + 1]
                out = jax.nn.gelu(t0[pl.ds(r0, ROWS), cols]) * sg
                br_ref[pl.ds(r0, ROWS), cols] = out.astype(BF16)
            return carry
        lax.fori_loop(0, n_slabs, body, 0)


def _win_index_a(j):
    return jnp.where(j == 3, 7, j)


def _win_index_b(j):
    return jnp.where(j == 3, 8, jnp.maximum(j, 1) + 2)


def _win_index_c(j):
    return jnp.clip(j, 1, 2) + 4


def _mix_prompt_call(l, x, modp, p):
    tm = TM_PROMPT
    tps = SEQ // tm
    n_tiles = M_PROMPT // tm

    def mod_spec(k):
        return pl.BlockSpec((None, None, None, 1, D_MODEL), lambda i, j: (l, i // tps, k, 0, 0))

    full3 = lambda shape: pl.BlockSpec((None,) + shape, lambda i, j: (l,) + (0,) * len(shape))
    kern = functools.partial(_mix_prompt_kernel, tm=tm, tiles_per_seq=tps)
    return pl.pallas_call(
        kern,
        grid=(n_tiles, 4),
        in_specs=[
            pl.BlockSpec((tm, D_MODEL), lambda i, j: (i, 0), pipeline_mode=pl.Buffered(1)),
            full3((1, D_MODEL)),
            mod_spec(0), mod_spec(1),
            pl.BlockSpec((None, D_MODEL, CB), lambda i, j: (l, 0, _win_index_a(j))),
            pl.BlockSpec((None, D_MODEL, CB), lambda i, j: (l, 0, _win_index_b(j))),
            pl.BlockSpec((None, D_MODEL, CB), lambda i, j: (l, 0, _win_index_c(j))),
            full3((4, GROUP, GROUP)),
            full3((1, POOL_WIDTH)),
            pl.BlockSpec((None, CONV_K, CB), lambda i, j: (l, 0, jnp.clip(j - 1, 0, 1))),
            full3((1, SGU_WIDTH)),
            full3((4, CHUNK, CHUNK)),
            full3((CHUNK, 4)),
        ],
        out_specs=[
            pl.BlockSpec((tm, CB), lambda i, j: (i, j)),
            pl.BlockSpec((None, POOL_HALO, CB), lambda i, j: (i, 0, 0)),
            pl.BlockSpec((None, CONV_HALO, CB), lambda i, j: (i, 0, jnp.clip(j - 1, 0, 1))),
            pl.BlockSpec((CHUNK, SGU_WIDTH), lambda i, j: (i, 0)),
        ],
        out_shape=[
            jax.ShapeDtypeStruct((M_PROMPT, D_MODEL), BF16),
            jax.ShapeDtypeStruct((n_tiles, POOL_HALO, POOL_WIDTH), F32),
            jax.ShapeDtypeStruct((n_tiles, CONV_HALO, CONV_WIDTH), F32),
            jax.ShapeDtypeStruct((n_tiles * CHUNK, SGU_WIDTH), F32),
        ],
        scratch_shapes=[
            pltpu.VMEM((tm, D_MODEL), BF16),
            pltpu.VMEM((tm, CB), F32),
            pltpu.VMEM((tm, CB), F32),
            pltpu.VMEM((tm + POOL_HALO, CB), F32),
            pltpu.VMEM((tm + CONV_HALO, CB), F32),
            pltpu.VMEM((CONV_WIDTH // CB, CONV_HALO, CB), F32),
        ],
        compiler_params=pltpu.CompilerParams(
            dimension_semantics=("arbitrary", "arbitrary"), vmem_limit_bytes=VMEM_LIMIT),
        name="mix_prompt",
    )(x, p["norm1"], modp, modp, p["w_in"], p["w_in"], p["w_in"], p["w_pool_grp"], p["pool_scale"],
      p["w_conv"], p["sgu_norm"], p["w_sgu"], p["b_sgu_t"])


def _mix_sample_kernel(x_ref, n1_ref, sh_ref, sc_ref, wa_ref, wb_ref, wc_ref, wgrp_ref, pscale_ref,
                       wconv_ref, sgun_ref, wv_ref, bv_ref, spool_ref, sconv_ref,
                       br_ref, pool_ref, conv_ref, v_ref,
                       hbuf, t0, t1):
    j = pl.program_id(0)
    nb = DEC_BATCH
    T = DEC_SEQ

    def slab(t):
        return slice(t * nb, (t + 1) * nb)

    @pl.when(j == 0)
    def _():
        _modulated_norm(x_ref, n1_ref, sc_ref, sh_ref, hbuf, T * nb)

    @pl.when(j == 0)
    def _pool():
        t0[...] = _dot(hbuf[...], wa_ref[...])
        keep = POOL_BUF - T
        pool_ref[0:keep * nb, :] = spool_ref[T * nb:POOL_BUF * nb, :]
        pool_ref[keep * nb:POOL_BUF * nb, :] = t0[...]

        def full(s, cols):
            if s < POOL_BUF:
                return spool_ref[slab(s), cols]
            return t0[slab(s - POOL_BUF), cols]

        for t in range(T):
            for g, w in enumerate(POOL_WINDOWS):
                cols = slice(g * GROUP, (g + 1) * GROUP)
                s = full(POOL_BUF + t, cols)
                for k in range(1, w):
                    s = s + full(POOL_BUF + t - k, cols)
                cnt = float(min(PAST_LEN + t + 1, w))
                d = s / cnt - t0[slab(t), cols]
                y = _dot(d.astype(BF16), wgrp_ref[g].astype(BF16)) * pscale_ref[:, cols]
                br_ref[slab(t), cols] = y.astype(BF16)

    @pl.when(jnp.logical_and(j >= 1, j <= 2))
    def _conv():
        h = hbuf[...]
        t1[...] = _dot(h, wc_ref[...]) * _dot(h, wa_ref[...])
        t0[...] = _dot(h, wb_ref[...])

        def full(s):
            if s < CONV_BUF:
                return sconv_ref[slab(s), :]
            return t1[slab(s - CONV_BUF), :]

        for s in range(CONV_BUF):
            conv_ref[slab(s), :] = full(T + s)
        for t in range(T):
            conv = (wconv_ref[0:1, :] * full(t) + wconv_ref[1:2, :] * full(t + 1)
                    + wconv_ref[2:3, :] * full(t + 2))
            br_ref[slab(t), :] = (t0[slab(t), :] * conv).astype(BF16)

    @pl.when(j == 3)
    def _sgu():
        h = hbuf[...]
        t0[...] = _dot(h, wa_ref[...])
        t1[...] = _dot(h, wb_ref[...])
        for t in range(T):
            v_ref[slab(t), :] = _rms_scale(jax.nn.gelu(t1[slab(t), :]), sgun_ref[...])
        for t in range(T):
            sg = bv_ref[t:t + 1, :]
            for k in range(t + 1):
                sg = sg + wv_ref[t * T + k:t * T + k + 1, :] * v_ref[slab(k), :]
            br_ref[slab(t), :] = (jax.nn.gelu(t0[slab(t), :]) * sg).astype(BF16)


def _mix_sample_call(l, x, mod, p, state_pool_tm, state_conv_tm):
    tm = M_SAMPLE

    def mod_spec(k):
        return pl.BlockSpec((None, DEC_BATCH, D_MODEL), lambda j: (l, 0, k))

    full3 = lambda shape: pl.BlockSpec((None,) + shape, lambda j: (l,) + (0,) * len(shape))
    cbi = lambda j: jnp.clip(j - 1, 0, 1)
    return pl.pallas_call(
        _mix_sample_kernel,
        grid=(4,),
        in_specs=[
            pl.BlockSpec((tm, D_MODEL), lambda j: (0, 0)),
            full3((1, D_MODEL)),
            mod_spec(0), mod_spec(1),
            pl.BlockSpec((None, D_MODEL, CB), lambda j: (l, 0, _win_index_a(j))),
            pl.BlockSpec((None, D_MODEL, CB), lambda j: (l, 0, _win_index_b(j))),
            pl.BlockSpec((None, D_MODEL, CB), lambda j: (l, 0, _win_index_c(j))),
            full3((4, GROUP, GROUP)),
            full3((1, POOL_WIDTH)),
            pl.BlockSpec((None, CONV_K, CB), lambda j: (l, 0, cbi(j))),
            full3((1, SGU_WIDTH)),
            full3((DEC_SEQ * DEC_SEQ, SGU_WIDTH)),
            full3((DEC_SEQ, SGU_WIDTH)),
            full3((POOL_BUF * DEC_BATCH, POOL_WIDTH)),
            pl.BlockSpec((None, CONV_BUF * DEC_BATCH, CB), lambda j: (l, 0, cbi(j))),
        ],
        out_specs=[
            pl.BlockSpec((tm, CB), lambda j: (0, j)),
            pl.BlockSpec((POOL_BUF * DEC_BATCH, POOL_WIDTH), lambda j: (0, 0)),
            pl.BlockSpec((CONV_BUF * DEC_BATCH, CB), lambda j: (0, cbi(j))),
            pl.BlockSpec((tm, SGU_WIDTH), lambda j: (0, 0)),
        ],
        out_shape=[
            jax.ShapeDtypeStruct((tm, D_MODEL), BF16),
            jax.ShapeDtypeStruct((POOL_BUF * DEC_BATCH, POOL_WIDTH), F32),
            jax.ShapeDtypeStruct((CONV_BUF * DEC_BATCH, CONV_WIDTH), F32),
            jax.ShapeDtypeStruct((tm, SGU_WIDTH), F32),
        ],
        scratch_shapes=[
            pltpu.VMEM((tm, D_MODEL), BF16),
            pltpu.VMEM((tm, CB), F32),
            pltpu.VMEM((tm, CB), F32),
        ],
        compiler_params=pltpu.CompilerParams(
            dimension_semantics=("arbitrary",), vmem_limit_bytes=VMEM_LIMIT),
        name="mix_sample",
    )(x, p["norm1"], mod, mod, p["w_in"], p["w_in"], p["w_in"], p["w_pool_grp"], p["pool_scale"],
      p["w_conv"], p["sgu_norm"], p["w_sgu_v"], p["b_sgu_v"], state_pool_tm, state_conv_tm)


def _gate_kernel(x_ref, br_ref, n1_ref, sh_ref, sc_ref, g_ref, wg0_ref, wg1_ref, wg2_ref,
                 wbp_ref, wbc_ref, wbs_ref, wout_ref, o_ref, hbuf, mbuf, *, tm, n1):
    j = pl.program_id(1)

    @pl.when(j == 0)
    def _():
        _modulated_norm(x_ref, n1_ref, sc_ref, sh_ref, hbuf, tm)

    @pl.when(j < n1)
    def _merge():
        h = hbuf[...]
        m = jax.nn.sigmoid(_dot(h, wg0_ref[...])) * _dot(br_ref[:, 0:POOL_WIDTH], wbp_ref[...])
        m = m + (jax.nn.sigmoid(_dot(h, wg1_ref[...]))
                 * _dot(br_ref[:, POOL_WIDTH:POOL_WIDTH + CONV_WIDTH], wbc_ref[...]))
        m = m + (jax.nn.sigmoid(_dot(h, wg2_ref[...]))
                 * _dot(br_ref[:, POOL_WIDTH + CONV_WIDTH:], wbs_ref[...]))
        mbuf[j] = m.astype(BF16)

    @pl.when(j >= n1)
    def _out():
        o = _dot(mbuf[0], wout_ref[0:TN_GATE, :])
        for k in range(1, n1):
            o = o + _dot(mbuf[k], wout_ref[k * TN_GATE:(k + 1) * TN_GATE, :])
        for kt in range(D_MODEL // TN_OUT):
            cols = slice(kt * TN_OUT, (kt + 1) * TN_OUT)

            @pl.when(j == n1 + kt)
            def _():
                for r in range(tm // ROWS):
                    rows = slice(r * ROWS, (r + 1) * ROWS)
                    o_ref[rows, :] = x_ref[rows, cols] + g_ref[:, cols] * o[rows, :]


def _gate_call(l, x, br, p, grp):
    tm, n_tiles, mod_arr, mod_spec = grp["tm"], grp["n_tiles"], grp["mod"], grp["mod_spec"]
    n1 = D_MODEL // TN_GATE
    n2 = D_MODEL // TN_OUT
    gate0 = OFF_GATE // TN_GATE
    jm = lambda j: jnp.minimum(j, n1 - 1)

    def gate_spec(k):
        return pl.BlockSpec((None, D_MODEL, TN_GATE), lambda i, j: (l, 0, gate0 + k * n1 + jm(j)))

    def branch_spec(width):
        return pl.BlockSpec((None, width, TN_GATE), lambda i, j: (l, 0, jm(j)))

    kern = functools.partial(_gate_kernel, tm=tm, n1=n1)
    return pl.pallas_call(
        kern,
        grid=(n_tiles, n1 + n2),
        in_specs=[
            pl.BlockSpec((tm, D_MODEL), lambda i, j: (i, 0), pipeline_mode=pl.Buffered(1)),
            pl.BlockSpec((tm, D_MODEL), lambda i, j: (i, 0), pipeline_mode=pl.Buffered(1)),
            pl.BlockSpec((None, 1, D_MODEL), lambda i, j: (l, 0, 0)),
            mod_spec(l, 0), mod_spec(l, 1), mod_spec(l, 2),
            gate_spec(0), gate_spec(1), gate_spec(2),
            branch_spec(POOL_WIDTH), branch_spec(CONV_WIDTH), branch_spec(SGU_WIDTH),
            pl.BlockSpec((None, D_MODEL, TN_OUT), lambda i, j: (l, 0, jnp.maximum(j - n1, 0))),
        ],
        out_specs=pl.BlockSpec((tm, TN_OUT), lambda i, j: (i, jnp.maximum(j - n1, 0))),
        out_shape=jax.ShapeDtypeStruct(x.shape, F32),
        scratch_shapes=[
            pltpu.VMEM((tm, D_MODEL), BF16),
            pltpu.VMEM((n1, tm, TN_GATE), BF16),
        ],
        compiler_params=pltpu.CompilerParams(
            dimension_semantics=("arbitrary", "arbitrary"), vmem_limit_bytes=VMEM_LIMIT),
        name="gate_" + grp["name"],
    )(x, br, p["norm1"], mod_arr, mod_arr, mod_arr, p["w_in"], p["w_in"], p["w_in"],
      p["w_br_pool"], p["w_br_conv"], p["w_br_sgu"], p["w_out"])


def _ffn_kernel(x_ref, n2_ref, sh_ref, sc_ref, g_ref, fn_ref, w1_ref, w2_ref, o_ref, hbuf,
                *, tm, nf, final_norm):
    f = pl.program_id(1)
    half = tm // 2

    @pl.when(f == 0)
    def _():
        _modulated_norm(x_ref, n2_ref, sc_ref, sh_ref, hbuf, tm)
        o_ref[...] = jnp.zeros((tm, D_MODEL), F32)

    for hh in range(2):
        rows = slice(hh * half, (hh + 1) * half)
        a = jnp.square(jax.nn.relu(_dot(hbuf[rows, :], w1_ref[...]))).astype(BF16)
        o_ref[rows, :] += _dot(a, w2_ref[...])

    @pl.when(f == nf - 1)
    def _():
        def body(c, carry):
            r0 = pl.multiple_of(c * ROWS, ROWS)
            y = x_ref[pl.ds(r0, ROWS), :] + g_ref[...] * o_ref[pl.ds(r0, ROWS), :]
            if final_norm:
                y = _rms_scale(y, fn_ref[...])
            o_ref[pl.ds(r0, ROWS), :] = y
            return carry
        lax.fori_loop(0, tm // ROWS, body, 0)


def _ffn_call(l, x, p, grp, final_norm):
    tm, n_tiles, mod_arr, mod_spec = grp["tm"], grp["n_tiles"], grp["mod"], grp["mod_spec"]
    nf = D_FF // TF
    kern = functools.partial(_ffn_kernel, tm=tm, nf=nf, final_norm=final_norm)
    return pl.pallas_call(
        kern,
        grid=(n_tiles, nf),
        in_specs=[
            pl.BlockSpec((tm, D_MODEL), lambda i, f: (i, 0), pipeline_mode=pl.Buffered(1)),
            pl.BlockSpec((None, 1, D_MODEL), lambda i, f: (l, 0, 0)),
            mod_spec(l, 3), mod_spec(l, 4), mod_spec(l, 5),
            pl.BlockSpec((1, D_MODEL), lambda i, f: (0, 0)),
            pl.BlockSpec((None, D_MODEL, TF), lambda i, f: (l, 0, f)),
            pl.BlockSpec((None, TF, D_MODEL), lambda i, f: (l, f, 0)),
        ],
        out_specs=pl.BlockSpec((tm, D_MODEL), lambda i, f: (i, 0)),
        out_shape=jax.ShapeDtypeStruct(x.shape, F32),
        scratch_shapes=[pltpu.VMEM((tm, D_MODEL), BF16)],
        compiler_params=pltpu.CompilerParams(
            dimension_semantics=("arbitrary", "arbitrary"), vmem_limit_bytes=VMEM_LIMIT),
        name="ffn_" + grp["name"],
    )(x, p["norm2"], mod_arr, mod_arr, mod_arr, p["final_norm"], p["w_ff1"], p["w_ff2"])


def kernel(x_prompt, x_sample, state_pool, state_conv, c_prompt, c_sample, norm1, norm2, w_ada, b_ada,
           w_in, w_pool_grp, pool_scale, w_conv, sgu_norm, w_sgu, b_sgu, w_br_pool, w_br_conv,
           w_br_sgu, w_out, w_ff1, w_ff2, final_norm):
    T, nb = DEC_SEQ, DEC_BATCH
    tps = SEQ // TM_PROMPT

    c_all = jnp.concatenate(
        [c_sample, c_prompt, jnp.zeros((ADA_ROWS - nb - BATCH, D_MODEL), F32)], axis=0)
    mod = _ada_call(c_all, w_ada, b_ada)
    modp = mod[:, nb:nb + BATCH].reshape(DEPTH, BATCH, N_MOD, 1, D_MODEL)

    p = {
        "norm1": norm1.reshape(DEPTH, 1, D_MODEL),
        "norm2": norm2.reshape(DEPTH, 1, D_MODEL),
        "final_norm": final_norm.reshape(1, D_MODEL),
        "w_in": w_in.astype(BF16),
        "w_pool_grp": w_pool_grp,
        "pool_scale": pool_scale.reshape(DEPTH, 1, POOL_WIDTH),
        "w_conv": w_conv,
        "sgu_norm": sgu_norm.reshape(DEPTH, 1, SGU_WIDTH),
        "w_sgu": w_sgu,
        "b_sgu_t": jnp.transpose(b_sgu, (0, 2, 1)),
        "w_sgu_v": jnp.repeat(
            jnp.transpose(w_sgu[:, :, :T, :T], (0, 2, 3, 1)).reshape(DEPTH, T * T, 4), GROUP, axis=-1),
        "b_sgu_v": jnp.repeat(jnp.transpose(b_sgu[:, :, :T], (0, 2, 1)), GROUP, axis=-1),
        "w_br_pool": w_br_pool.astype(BF16),
        "w_br_conv": w_br_conv.astype(BF16),
        "w_br_sgu": w_br_sgu.astype(BF16),
        "w_out": w_out.astype(BF16),
        "w_ff1": w_ff1.astype(BF16),
        "w_ff2": w_ff2.astype(BF16),
    }

    grp_p = {
        "name": "prompt", "tm": TM_PROMPT, "n_tiles": M_PROMPT // TM_PROMPT, "mod": modp,
        "mod_spec": lambda l, k: pl.BlockSpec(
            (None, None, None, 1, D_MODEL), lambda i, j: (l, i // tps, k, 0, 0)),
    }
    grp_s = {
        "name": "sample", "tm": M_SAMPLE, "n_tiles": 1, "mod": mod,
        "mod_spec": lambda l, k: pl.BlockSpec((None, nb, D_MODEL), lambda i, j: (l, 0, k)),
    }

    xp = x_prompt.reshape(M_PROMPT, D_MODEL)
    xs = jnp.transpose(x_sample, (1, 0, 2)).reshape(M_SAMPLE, D_MODEL)
    spool_tm = jnp.transpose(state_pool, (0, 2, 1, 3)).reshape(DEPTH, POOL_BUF * nb, POOL_WIDTH)
    sconv_tm = jnp.transpose(state_conv, (0, 2, 1, 3)).reshape(DEPTH, CONV_BUF * nb, CONV_WIDTH)

    pool_p, conv_p, v_p, pool_s, conv_s, v_s = [], [], [], [], [], []
    last_tile = slice(tps - 1, None, tps)
    for l in range(DEPTH):
        final = l == DEPTH - 1
        br, ptail, ztail, vtail = _mix_prompt_call(l, xp, modp, p)
        xp = _gate_call(l, xp, br, p, grp_p)
        xp = _ffn_call(l, xp, p, grp_p, final)
        pool_p.append(ptail[last_tile, POOL_HALO - POOL_BUF:, :])
        conv_p.append(ztail[last_tile, CONV_HALO - CONV_BUF:, :])
        v_p.append(vtail.reshape(-1, CHUNK, SGU_WIDTH)[last_tile])

        br, npool, nconv, nv = _mix_sample_call(l, xs, mod, p, spool_tm, sconv_tm)
        xs = _gate_call(l, xs, br, p, grp_s)
        xs = _ffn_call(l, xs, p, grp_s, final)
        pool_s.append(jnp.transpose(npool.reshape(POOL_BUF, nb, POOL_WIDTH), (1, 0, 2)))
        conv_s.append(jnp.transpose(nconv.reshape(CONV_BUF, nb, CONV_WIDTH), (1, 0, 2)))
        v_s.append(jnp.transpose(nv.reshape(T, nb, SGU_WIDTH), (1, 0, 2)))

    y_prompt = xp.reshape(BATCH, SEQ, D_MODEL)
    y_sample = jnp.transpose(xs.reshape(T, nb, D_MODEL), (1, 0, 2))
    return (y_prompt, y_sample, jnp.stack(pool_p), jnp.stack(conv_p), jnp.stack(v_p),
            jnp.stack(pool_s), jnp.stack(conv_s), jnp.stack(v_s))
```

```python
import functools

import jax
import jax.numpy as jnp
from jax import lax
from jax.experimental import pallas as pl
from jax.experimental.pallas import tpu as pltpu

F32 = jnp.float32
BF16 = jnp.bfloat16

D_MODEL = 2048
DEPTH = 4
BATCH = 4
SEQ = 2048
DEC_BATCH = 128
DEC_SEQ = 4
PAST_LEN = 16384
POOL_WINDOWS = (2, 4, 8, 16)
GROUP = 128
POOL_WIDTH = 512
POOL_BUF = 15
CONV_WIDTH = 1024
CONV_K = 3
CONV_BUF = 2
CHUNK = 128
SGU_WIDTH = 512
N_MOD = 6
D_FF = 4 * D_MODEL
EPS = 1e-6
OFF_GATE = POOL_WIDTH + 3 * CONV_WIDTH + 2 * SGU_WIDTH
N_IN = OFF_GATE + 3 * D_MODEL

ROWS = 128
CB = 512
POOL_HALO = 16
CONV_HALO = 8
TM_PROMPT = 1024
M_PROMPT = BATCH * SEQ
M_SAMPLE = DEC_BATCH * DEC_SEQ
TN_GATE = 512
TN_OUT = 512
TF = 1024
TN_ADA = 1024
ADA_ROWS = 136
VMEM_LIMIT = 60 * 1024 * 1024


def _dot(a, b):
    return jnp.dot(a, b, preferred_element_type=F32)


def _rms_scale(x, g):
    return x * lax.rsqrt(jnp.mean(x * x, axis=-1, keepdims=True) + EPS) * g


def _modulated_norm(x_ref, g_ref, sc_ref, sh_ref, h_ref, tm):
    def body(c, carry):
        r0 = pl.multiple_of(c * ROWS, ROWS)
        x = x_ref[pl.ds(r0, ROWS), :]
        r = lax.rsqrt(jnp.mean(x * x, axis=-1, keepdims=True) + EPS)
        h = x_ref[pl.ds(r0, ROWS), :] * r * (g_ref[...] * (1.0 + sc_ref[...])) + sh_ref[...]
        h_ref[pl.ds(r0, ROWS), :] = h.astype(BF16)
        return carry
    lax.fori_loop(0, tm // ROWS, body, 0)


def _ada_kernel(c_ref, w_ref, b_ref, o_ref):
    a = jax.nn.silu(c_ref[...]).astype(BF16)
    o_ref[...] = _dot(a, w_ref[...].astype(BF16)) + b_ref[...]


def _ada_call(c_all, w_ada, b_ada):
    n = N_MOD * D_MODEL
    return pl.pallas_call(
        _ada_kernel,
        grid=(DEPTH, n // TN_ADA),
        in_specs=[
            pl.BlockSpec((ADA_ROWS, D_MODEL), lambda l, j: (0, 0)),
            pl.BlockSpec((None, D_MODEL, TN_ADA), lambda l, j: (l, 0, j)),
            pl.BlockSpec((None, 1, TN_ADA), lambda l, j: (l, 0, j)),
        ],
        out_specs=pl.BlockSpec((None, ADA_ROWS, TN_ADA), lambda l, j: (l, 0, j)),
        out_shape=jax.ShapeDtypeStruct((DEPTH, ADA_ROWS, n), F32),
        compiler_params=pltpu.CompilerParams(
            dimension_semantics=("arbitrary", "arbitrary"), vmem_limit_bytes=VMEM_LIMIT),
        name="ada",
    )(c_all, w_ada, b_ada.reshape(DEPTH, 1, n))


def _mix_prompt_kernel(x_ref, n1_ref, sh_ref, sc_ref, wa_ref, wb_ref, wc_ref, wgrp_ref, pscale_ref,
                       wconv_ref, sgun_ref, wsgu_ref, bsgu_ref,
                       br_ref, ptail_ref, ztail_ref, vtail_ref,
                       hbuf, t0, t1, pbuf, zbuf, zhalo, *, tm, tiles_per_seq):
    i = pl.program_id(0)
    j = pl.program_id(1)
    tile_in_seq = i % tiles_per_seq
    first_tile = tile_in_seq == 0
    n_slabs = tm // ROWS

    @pl.when(j == 0)
    def _():
        _modulated_norm(x_ref, n1_ref, sc_ref, sh_ref, hbuf, tm)

    @pl.when(j == 0)
    def _pool():
        @pl.when(first_tile)
        def _():
            pbuf[0:POOL_HALO, :] = jnp.zeros((POOL_HALO, CB), F32)

        @pl.when(jnp.logical_not(first_tile))
        def _():
            pbuf[0:POOL_HALO, :] = pbuf[tm:tm + POOL_HALO, :]

        pbuf[POOL_HALO:POOL_HALO + tm, :] = _dot(hbuf[...], wa_ref[...])
        ptail_ref[...] = pbuf[tm:tm + POOL_HALO, :]
        pos0 = tile_in_seq * tm

        def body(c, carry):
            r0 = pl.multiple_of(c * ROWS, ROWS)
            pos = pos0 + r0 + lax.broadcasted_iota(jnp.int32, (ROWS, 1), 0)
            for g, w in enumerate(POOL_WINDOWS):
                cols = slice(g * GROUP, (g + 1) * GROUP)
                ext = pbuf[pl.ds(r0, ROWS + POOL_HALO), cols]
                s = ext
                k = 1
                while k < w:
                    s = s + pltpu.roll(s, k, axis=0)
                    k *= 2
                cnt = jnp.minimum(pos + 1, w).astype(F32)
                d = s[POOL_HALO:, :] / cnt - ext[POOL_HALO:, :]
                y = _dot(d.astype(BF16), wgrp_ref[g].astype(BF16)) * pscale_ref[:, cols]
                br_ref[pl.ds(r0, ROWS), cols] = y.astype(BF16)
            return carry
        lax.fori_loop(0, n_slabs, body, 0)

    @pl.when(jnp.logical_and(j >= 1, j <= 2))
    def _conv():
        cb = j - 1

        @pl.when(first_tile)
        def _():
            zbuf[0:CONV_HALO, :] = jnp.zeros((CONV_HALO, CB), F32)

        @pl.when(jnp.logical_not(first_tile))
        def _():
            zbuf[0:CONV_HALO, :] = zhalo[cb]

        h = hbuf[...]
        zbuf[CONV_HALO:CONV_HALO + tm, :] = _dot(h, wc_ref[...]) * _dot(h, wa_ref[...])
        t0[...] = _dot(h, wb_ref[...])
        tail = zbuf[tm:tm + CONV_HALO, :]
        zhalo[cb] = tail
        ztail_ref[...] = tail
        w0 = wconv_ref[0:1, :]
        w1 = wconv_ref[1:2, :]
        w2 = wconv_ref[2:3, :]

        def body(c, carry):
            r0 = pl.multiple_of(c * ROWS, ROWS)
            ext = zbuf[pl.ds(r0, ROWS + CONV_HALO), :]
            conv = (w0 * pltpu.roll(ext, 2, axis=0)[CONV_HALO:, :]
                    + w1 * pltpu.roll(ext, 1, axis=0)[CONV_HALO:, :]
                    + w2 * ext[CONV_HALO:, :])
            br_ref[pl.ds(r0, ROWS), :] = (t0[pl.ds(r0, ROWS), :] * conv).astype(BF16)
            return carry
        lax.fori_loop(0, n_slabs, body, 0)

    @pl.when(j == 3)
    def _sgu():
        h = hbuf[...]
        t0[...] = _dot(h, wa_ref[...])
        t1[...] = _dot(h, wb_ref[...])
        row = lax.broadcasted_iota(jnp.int32, (CHUNK, CHUNK), 0)
        col = lax.broadcasted_iota(jnp.int32, (CHUNK, CHUNK), 1)
        causal = row >= col

        def body(c, carry):
            r0 = pl.multiple_of(c * ROWS, ROWS)
            vn = _rms_scale(jax.nn.gelu(t1[pl.ds(r0, ROWS), :]), sgun_ref[...])

            @pl.when(c == n_slabs - 1)
            def _():
                vtail_ref[...] = vn

            for g in range(SGU_WIDTH // GROUP):
                cols = slice(g * GROUP, (g + 1) * GROUP)
                wt = jnp.where(causal, wsgu_ref[g], 0.0).astype(BF16)
                sg = _dot(wt, vn[:, cols].astype(BF16)) + bsgu_ref[:, g:g + 1]
                out = jax.nn.gelu(t0[pl.ds(r0, ROWS), cols]) * sg
                br_ref[pl.ds(r0, ROWS), cols] = out.astype(BF16)
            return carry
        lax.fori_loop(0, n_slabs, body, 0)


def _win_index_a(j):
    return jnp.where(j == 3, 7, j)


def _win_index_b(j):
    return jnp.where(j == 3, 8, jnp.maximum(j, 1) + 2)


def _win_index_c(j):
    return jnp.clip(j, 1, 2) + 4


def _mix_prompt_call(l, x, modp, p):
    tm = TM_PROMPT
    tps = SEQ // tm
    n_tiles = M_PROMPT // tm

    def mod_spec(k):
        return pl.BlockSpec((None, None, None, 1, D_MODEL), lambda i, j: (l, i // tps, k, 0, 0))

    full3 = lambda shape: pl.BlockSpec((None,) + shape, lambda i, j: (l,) + (0,) * len(shape))
    kern = functools.partial(_mix_prompt_kernel, tm=tm, tiles_per_seq=tps)
    return pl.pallas_call(
        kern,
        grid=(n_tiles, 4),
        in_specs=[
            pl.BlockSpec((tm, D_MODEL), lambda i, j: (i, 0)),
            full3((1, D_MODEL)),
            mod_spec(0), mod_spec(1),
            pl.BlockSpec((None, D_MODEL, CB), lambda i, j: (l, 0, _win_index_a(j))),
            pl.BlockSpec((None, D_MODEL, CB), lambda i, j: (l, 0, _win_index_b(j))),
            pl.BlockSpec((None, D_MODEL, CB), lambda i, j: (l, 0, _win_index_c(j))),
            full3((4, GROUP, GROUP)),
            full3((1, POOL_WIDTH)),
            pl.BlockSpec((None, CONV_K, CB), lambda i, j: (l, 0, jnp.clip(j - 1, 0, 1))),
            full3((1, SGU_WIDTH)),
            full3((4, CHUNK, CHUNK)),
            full3((CHUNK, 4)),
        ],
        out_specs=[
            pl.BlockSpec((tm, CB), lambda i, j: (i, j)),
            pl.BlockSpec((None, POOL_HALO, CB), lambda i, j: (i, 0, 0)),
            pl.BlockSpec((None, CONV_HALO, CB), lambda i, j: (i, 0, jnp.clip(j - 1, 0, 1))),
            pl.BlockSpec((CHUNK, SGU_WIDTH), lambda i, j: (i, 0)),
        ],
        out_shape=[
            jax.ShapeDtypeStruct((M_PROMPT, D_MODEL), BF16),
            jax.ShapeDtypeStruct((n_tiles, POOL_HALO, POOL_WIDTH), F32),
            jax.ShapeDtypeStruct((n_tiles, CONV_HALO, CONV_WIDTH), F32),
            jax.ShapeDtypeStruct((n_tiles * CHUNK, SGU_WIDTH), F32),
        ],
        scratch_shapes=[
            pltpu.VMEM((tm, D_MODEL), BF16),
            pltpu.VMEM((tm, CB), F32),
            pltpu.VMEM((tm, CB), F32),
            pltpu.VMEM((tm + POOL_HALO, CB), F32),
            pltpu.VMEM((tm + CONV_HALO, CB), F32),
            pltpu.VMEM((CONV_WIDTH // CB, CONV_HALO, CB), F32),
        ],
        compiler_params=pltpu.CompilerParams(
            dimension_semantics=("arbitrary", "arbitrary"), vmem_limit_bytes=VMEM_LIMIT),
        name="mix_prompt",
    )(x, p["norm1"], modp, modp, p["w_in"], p["w_in"], p["w_in"], p["w_pool_grp"], p["pool_scale"],
      p["w_conv"], p["sgu_norm"], p["w_sgu"], p["b_sgu_t"])


def _mix_sample_kernel(x_ref, n1_ref, sh_ref, sc_ref, wa_ref, wb_ref, wc_ref, wgrp_ref, pscale_ref,
                       wconv_ref, sgun_ref, wv_ref, bv_ref, spool_ref, sconv_ref,
                       br_ref, pool_ref, conv_ref, v_ref,
                       hbuf, t0, t1):
    j = pl.program_id(0)
    nb = DEC_BATCH
    T = DEC_SEQ

    def slab(t):
        return slice(t * nb, (t + 1) * nb)

    @pl.when(j == 0)
    def _():
        _modulated_norm(x_ref, n1_ref, sc_ref, sh_ref, hbuf, T * nb)

    @pl.when(j == 0)
    def _pool():
        t0[...] = _dot(hbuf[...], wa_ref[...])
        keep = POOL_BUF - T
        pool_ref[0:keep * nb, :] = spool_ref[T * nb:POOL_BUF * nb, :]
        pool_ref[keep * nb:POOL_BUF * nb, :] = t0[...]

        def full(s, cols):
            if s < POOL_BUF:
                return spool_ref[slab(s), cols]
            return t0[slab(s - POOL_BUF), cols]

        for t in range(T):
            for g, w in enumerate(POOL_WINDOWS):
                cols = slice(g * GROUP, (g + 1) * GROUP)
                s = full(POOL_BUF + t, cols)
                for k in range(1, w):
                    s = s + full(POOL_BUF + t - k, cols)
                cnt = float(min(PAST_LEN + t + 1, w))
                d = s / cnt - t0[slab(t), cols]
                y = _dot(d.astype(BF16), wgrp_ref[g].astype(BF16)) * pscale_ref[:, cols]
                br_ref[slab(t), cols] = y.astype(BF16)

    @pl.when(jnp.logical_and(j >= 1, j <= 2))
    def _conv():
        h = hbuf[...]
        t1[...] = _dot(h, wc_ref[...]) * _dot(h, wa_ref[...])
        t0[...] = _dot(h, wb_ref[...])

        def full(s):
            if s < CONV_BUF:
                return sconv_ref[slab(s), :]
            return t1[slab(s - CONV_BUF), :]

        for s in range(CONV_BUF):
            conv_ref[slab(s), :] = full(T + s)
        for t in range(T):
            conv = (wconv_ref[0:1, :] * full(t) + wconv_ref[1:2, :] * full(t + 1)
                    + wconv_ref[2:3, :] * full(t + 2))
            br_ref[slab(t), :] = (t0[slab(t), :] * conv).astype(BF16)

    @pl.when(j == 3)
    def _sgu():
        h = hbuf[...]
        t0[...] = _dot(h, wa_ref[...])
        t1[...] = _dot(h, wb_ref[...])
        for t in range(T):
            v_ref[slab(t), :] = _rms_scale(jax.nn.gelu(t1[slab(t), :]), sgun_ref[...])
        for t in range(T):
            sg = bv_ref[t:t + 1, :]
            for k in range(t + 1):
                sg = sg + wv_ref[t * T + k:t * T + k + 1, :] * v_ref[slab(k), :]
            br_ref[slab(t), :] = (jax.nn.gelu(t0[slab(t), :]) * sg).astype(BF16)


def _mix_sample_call(l, x, mod, p, state_pool_tm, state_conv_tm):
    tm = M_SAMPLE

    def mod_spec(k):
        return pl.BlockSpec((None, DEC_BATCH, D_MODEL), lambda j: (l, 0, k))

    full3 = lambda shape: pl.BlockSpec((None,) + shape, lambda j: (l,) + (0,) * len(shape))
    cbi = lambda j: jnp.clip(j - 1, 0, 1)
    return pl.pallas_call(
        _mix_sample_kernel,
        grid=(4,),
        in_specs=[
            pl.BlockSpec((tm, D_MODEL), lambda j: (0, 0)),
            full3((1, D_MODEL)),
            mod_spec(0), mod_spec(1),
            pl.BlockSpec((None, D_MODEL, CB), lambda j: (l, 0, _win_index_a(j))),
            pl.BlockSpec((None, D_MODEL, CB), lambda j: (l, 0, _win_index_b(j))),
            pl.BlockSpec((None, D_MODEL, CB), lambda j: (l, 0, _win_index_c(j))),
            full3((4, GROUP, GROUP)),
            full3((1, POOL_WIDTH)),
            pl.BlockSpec((None, CONV_K, CB), lambda j: (l, 0, cbi(j))),
            full3((1, SGU_WIDTH)),
            full3((DEC_SEQ * DEC_SEQ, SGU_WIDTH)),
            full3((DEC_SEQ, SGU_WIDTH)),
            full3((POOL_BUF * DEC_BATCH, POOL_WIDTH)),
            pl.BlockSpec((None, CONV_BUF * DEC_BATCH, CB), lambda j: (l, 0, cbi(j))),
        ],
        out_specs=[
            pl.BlockSpec((tm, CB), lambda j: (0, j)),
            pl.BlockSpec((POOL_BUF * DEC_BATCH, POOL_WIDTH), lambda j: (0, 0)),
            pl.BlockSpec((CONV_BUF * DEC_BATCH, CB), lambda j: (0, cbi(j))),
            pl.BlockSpec((tm, SGU_WIDTH), lambda j: (0, 0)),
        ],
        out_shape=[
            jax.ShapeDtypeStruct((tm, D_MODEL), BF16),
            jax.ShapeDtypeStruct((POOL_BUF * DEC_BATCH, POOL_WIDTH), F32),
            jax.ShapeDtypeStruct((CONV_BUF * DEC_BATCH, CONV_WIDTH), F32),
            jax.ShapeDtypeStruct((tm, SGU_WIDTH), F32),
        ],
        scratch_shapes=[
            pltpu.VMEM((tm, D_MODEL), BF16),
            pltpu.VMEM((tm, CB), F32),
            pltpu.VMEM((tm, CB), F32),
        ],
        compiler_params=pltpu.CompilerParams(
            dimension_semantics=("arbitrary",), vmem_limit_bytes=VMEM_LIMIT),
        name="mix_sample",
    )(x, p["norm1"], mod, mod, p["w_in"], p["w_in"], p["w_in"], p["w_pool_grp"], p["pool_scale"],
      p["w_conv"], p["sgu_norm"], p["w_sgu_v"], p["b_sgu_v"], state_pool_tm, state_conv_tm)


def _gate_kernel(x_ref, br_ref, n1_ref, sh_ref, sc_ref, xcol_ref, gcol_ref, wg0_ref, wg1_ref, wg2_ref,
                 wbp_ref, wbc_ref, wbs_ref, wout_ref, o_ref, hbuf, mbuf, *, tm, n1):
    j = pl.program_id(1)

    @pl.when(j == 0)
    def _():
        _modulated_norm(x_ref, n1_ref, sc_ref, sh_ref, hbuf, tm)

    @pl.when(j < n1)
    def _merge():
        h = hbuf[...]
        m = jax.nn.sigmoid(_dot(h, wg0_ref[...])) * _dot(br_ref[:, 0:POOL_WIDTH], wbp_ref[...])
        m = m + (jax.nn.sigmoid(_dot(h, wg1_ref[...]))
                 * _dot(br_ref[:, POOL_WIDTH:POOL_WIDTH + CONV_WIDTH], wbc_ref[...]))
        m = m + (jax.nn.sigmoid(_dot(h, wg2_ref[...]))
                 * _dot(br_ref[:, POOL_WIDTH + CONV_WIDTH:], wbs_ref[...]))
        m = m.astype(BF16)
        for jt in range(n1):
            @pl.when(j == jt)
            def _():
                mbuf[:, jt * TN_GATE:(jt + 1) * TN_GATE] = m

    @pl.when(j >= n1)
    def _out():
        o = _dot(mbuf[...], wout_ref[...])
        for r in range(tm // ROWS):
            rows = slice(r * ROWS, (r + 1) * ROWS)
            o_ref[rows, :] = xcol_ref[rows, :] + gcol_ref[...] * o[rows, :]


def _gate_call(l, x, br, p, grp):
    tm, n_tiles, mod_arr, mod_spec = grp["tm"], grp["n_tiles"], grp["mod"], grp["mod_spec"]
    n1 = D_MODEL // TN_GATE
    n2 = D_MODEL // TN_OUT
    gate0 = OFF_GATE // TN_GATE
    jm = lambda j: jnp.minimum(j, n1 - 1)

    def gate_spec(k):
        return pl.BlockSpec((None, D_MODEL, TN_GATE), lambda i, j: (l, 0, gate0 + k * n1 + jm(j)))

    def branch_spec(width):
        return pl.BlockSpec((None, width, TN_GATE), lambda i, j: (l, 0, jm(j)))

    kern = functools.partial(_gate_kernel, tm=tm, n1=n1)
    return pl.pallas_call(
        kern,
        grid=(n_tiles, n1 + n2),
        in_specs=[
            pl.BlockSpec((tm, D_MODEL), lambda i, j: (i, 0), pipeline_mode=pl.Buffered(1)),
            pl.BlockSpec((tm, D_MODEL), lambda i, j: (i, 0), pipeline_mode=pl.Buffered(1)),
            pl.BlockSpec((None, 1, D_MODEL), lambda i, j: (l, 0, 0)),
            mod_spec(l, 0), mod_spec(l, 1),
            pl.BlockSpec((tm, TN_OUT), lambda i, j: (i, jnp.maximum(j - n1, 0))),
            grp["mod_col_spec"](l, 2, TN_OUT, lambda j: jnp.maximum(j - n1, 0)),
            gate_spec(0), gate_spec(1), gate_spec(2),
            branch_spec(POOL_WIDTH), branch_spec(CONV_WIDTH), branch_spec(SGU_WIDTH),
            pl.BlockSpec((None, D_MODEL, TN_OUT), lambda i, j: (l, 0, jnp.maximum(j - n1, 0))),
        ],
        out_specs=pl.BlockSpec((tm, TN_OUT), lambda i, j: (i, jnp.maximum(j - n1, 0))),
        out_shape=jax.ShapeDtypeStruct(x.shape, F32),
        scratch_shapes=[
            pltpu.VMEM((tm, D_MODEL), BF16),
            pltpu.VMEM((tm, D_MODEL), BF16),
        ],
        compiler_params=pltpu.CompilerParams(
            dimension_semantics=("arbitrary", "arbitrary"), vmem_limit_bytes=VMEM_LIMIT),
        name="gate_" + grp["name"],
    )(x, br, p["norm1"], mod_arr, mod_arr, x, mod_arr, p["w_in"], p["w_in"], p["w_in"],
      p["w_br_pool"], p["w_br_conv"], p["w_br_sgu"], p["w_out"])


def _ffn_kernel(x_ref, n2_ref, sh_ref, sc_ref, g_ref, fn_ref, w1_ref, w2_ref, o_ref, hbuf,
                *, tm, nf, final_norm):
    f = pl.program_id(1)
    half = tm // 2

    @pl.when(f == 0)
    def _():
        _modulated_norm(x_ref, n2_ref, sc_ref, sh_ref, hbuf, tm)
        o_ref[...] = jnp.zeros((tm, D_MODEL), F32)

    for hh in range(2):
        rows = slice(hh * half, (hh + 1) * half)
        a = jnp.square(jax.nn.relu(_dot(hbuf[rows, :], w1_ref[...]))).astype(BF16)
        o_ref[rows, :] += _dot(a, w2_ref[...])

    @pl.when(f == nf - 1)
    def _():
        def body(c, carry):
            r0 = pl.multiple_of(c * ROWS, ROWS)
            y = x_ref[pl.ds(r0, ROWS), :] + g_ref[...] * o_ref[pl.ds(r0, ROWS), :]
            if final_norm:
                y = _rms_scale(y, fn_ref[...])
            o_ref[pl.ds(r0, ROWS), :] = y
            return carry
        lax.fori_loop(0, tm // ROWS, body, 0)


def _ffn_call(l, x, p, grp, final_norm):
    tm, n_tiles, mod_arr, mod_spec = grp["tm"], grp["n_tiles"], grp["mod"], grp["mod_spec"]
    nf = D_FF // TF
    kern = functools.partial(_ffn_kernel, tm=tm, nf=nf, final_norm=final_norm)
    return pl.pallas_call(
        kern,
        grid=(n_tiles, nf),
        in_specs=[
            pl.BlockSpec((tm, D_MODEL), lambda i, f: (i, 0)),
            pl.BlockSpec((None, 1, D_MODEL), lambda i, f: (l, 0, 0)),
            mod_spec(l, 3), mod_spec(l, 4), mod_spec(l, 5),
            pl.BlockSpec((1, D_MODEL), lambda i, f: (0, 0)),
            pl.BlockSpec((None, D_MODEL, TF), lambda i, f: (l, 0, f)),
            pl.BlockSpec((None, TF, D_MODEL), lambda i, f: (l, f, 0)),
        ],
        out_specs=pl.BlockSpec((tm, D_MODEL), lambda i, f: (i, 0)),
        out_shape=jax.ShapeDtypeStruct(x.shape, F32),
        scratch_shapes=[pltpu.VMEM((tm, D_MODEL), BF16)],
        compiler_params=pltpu.CompilerParams(
            dimension_semantics=("arbitrary", "arbitrary"), vmem_limit_bytes=VMEM_LIMIT),
        name="ffn_" + grp["name"],
    )(x, p["norm2"], mod_arr, mod_arr, mod_arr, p["final_norm"], p["w_ff1"], p["w_ff2"])


def kernel(x_prompt, x_sample, state_pool, state_conv, c_prompt, c_sample, norm1, norm2, w_ada, b_ada,
           w_in, w_pool_grp, pool_scale, w_conv, sgu_norm, w_sgu, b_sgu, w_br_pool, w_br_conv,
           w_br_sgu, w_out, w_ff1, w_ff2, final_norm):
    T, nb = DEC_SEQ, DEC_BATCH
    tps = SEQ // TM_PROMPT

    c_all = jnp.concatenate(
        [c_sample, c_prompt, jnp.zeros((ADA_ROWS - nb - BATCH, D_MODEL), F32)], axis=0)
    mod = _ada_call(c_all, w_ada, b_ada)
    modp = mod[:, nb:nb + BATCH].reshape(DEPTH, BATCH, N_MOD, 1, D_MODEL)

    p = {
        "norm1": norm1.reshape(DEPTH, 1, D_MODEL),
        "norm2": norm2.reshape(DEPTH, 1, D_MODEL),
        "final_norm": final_norm.reshape(1, D_MODEL),
        "w_in": w_in.astype(BF16),
        "w_pool_grp": w_pool_grp,
        "pool_scale": pool_scale.reshape(DEPTH, 1, POOL_WIDTH),
        "w_conv": w_conv,
        "sgu_norm": sgu_norm.reshape(DEPTH, 1, SGU_WIDTH),
        "w_sgu": w_sgu,
        "b_sgu_t": jnp.transpose(b_sgu, (0, 2, 1)),
        "w_sgu_v": jnp.repeat(
            jnp.transpose(w_sgu[:, :, :T, :T], (0, 2, 3, 1)).reshape(DEPTH, T * T, 4), GROUP, axis=-1),
        "b_sgu_v": jnp.repeat(jnp.transpose(b_sgu[:, :, :T], (0, 2, 1)), GROUP, axis=-1),
        "w_br_pool": w_br_pool.astype(BF16),
        "w_br_conv": w_br_conv.astype(BF16),
        "w_br_sgu": w_br_sgu.astype(BF16),
        "w_out": w_out.astype(BF16),
        "w_ff1": w_ff1.astype(BF16),
        "w_ff2": w_ff2.astype(BF16),
    }

    grp_p = {
        "name": "prompt", "tm": TM_PROMPT, "n_tiles": M_PROMPT // TM_PROMPT, "mod": modp,
        "mod_spec": lambda l, k: pl.BlockSpec(
            (None, None, None, 1, D_MODEL), lambda i, j: (l, i // tps, k, 0, 0)),
        "mod_col_spec": lambda l, k, width, col: pl.BlockSpec(
            (None, None, None, 1, width), lambda i, j: (l, i // tps, k, 0, col(j))),
    }
    grp_s = {
        "name": "sample", "tm": M_SAMPLE, "n_tiles": 1, "mod": mod,
        "mod_spec": lambda l, k: pl.BlockSpec((None, nb, D_MODEL), lambda i, j: (l, 0, k)),
        "mod_col_spec": lambda l, k, width, col: pl.BlockSpec(
            (None, nb, width), lambda i, j: (l, 0, k * (D_MODEL // width) + col(j))),
    }

    xp = x_prompt.reshape(M_PROMPT, D_MODEL)
    xs = jnp.transpose(x_sample, (1, 0, 2)).reshape(M_SAMPLE, D_MODEL)
    spool_tm = jnp.transpose(state_pool, (0, 2, 1, 3)).reshape(DEPTH, POOL_BUF * nb, POOL_WIDTH)
    sconv_tm = jnp.transpose(state_conv, (0, 2, 1, 3)).reshape(DEPTH, CONV_BUF * nb, CONV_WIDTH)

    pool_p, conv_p, v_p, pool_s, conv_s, v_s = [], [], [], [], [], []
    last_tile = slice(tps - 1, None, tps)
    for l in range(DEPTH):
        final = l == DEPTH - 1
        br, ptail, ztail, vtail = _mix_prompt_call(l, xp, modp, p)
        xp = _gate_call(l, xp, br, p, grp_p)
        xp = _ffn_call(l, xp, p, grp_p, final)
        pool_p.append(ptail[last_tile, POOL_HALO - POOL_BUF:, :])
        conv_p.append(ztail[last_tile, CONV_HALO - CONV_BUF:, :])
        v_p.append(vtail.reshape(-1, CHUNK, SGU_WIDTH)[last_tile])

        br, npool, nconv, nv = _mix_sample_call(l, xs, mod, p, spool_tm, sconv_tm)
        xs = _gate_call(l, xs, br, p, grp_s)
        xs = _ffn_call(l, xs, p, grp_s, final)
        pool_s.append(jnp.transpose(npool.reshape(POOL_BUF, nb, POOL_WIDTH), (1, 0, 2)))
        conv_s.append(jnp.transpose(nconv.reshape(CONV_BUF, nb, CONV_WIDTH), (1, 0, 2)))
        v_s.append(jnp.transpose(nv.reshape(T, nb, SGU_WIDTH), (1, 0, 2)))

    y_prompt = xp.reshape(BATCH, SEQ, D_MODEL)
    y_sample = jnp.transpose(xs.reshape(T, nb, D_MODEL), (1, 0, 2))
    return (y_prompt, y_sample, jnp.stack(pool_p), jnp.stack(conv_p), jnp.stack(v_p),
            jnp.stack(pool_s), jnp.stack(conv_s), jnp.stack(v_s))
```

```python
import functools

import jax
import jax.numpy as jnp
from jax import lax
from jax.experimental import pallas as pl
from jax.experimental.pallas import tpu as pltpu

F32 = jnp.float32
BF16 = jnp.bfloat16

D_MODEL = 2048
DEPTH = 4
BATCH = 4
SEQ = 2048
DEC_BATCH = 128
DEC_SEQ = 4
PAST_LEN = 16384
POOL_WINDOWS = (2, 4, 8, 16)
GROUP = 128
POOL_WIDTH = 512
POOL_BUF = 15
CONV_WIDTH = 1024
CONV_K = 3
CONV_BUF = 2
CHUNK = 128
SGU_WIDTH = 512
N_MOD = 6
D_FF = 4 * D_MODEL
EPS = 1e-6
OFF_GATE = POOL_WIDTH + 3 * CONV_WIDTH + 2 * SGU_WIDTH
N_IN = OFF_GATE + 3 * D_MODEL

ROWS = 128
CB = 512
POOL_HALO = 16
CONV_HALO = 8
TM_PROMPT = 1024
M_PROMPT = BATCH * SEQ
M_SAMPLE = DEC_BATCH * DEC_SEQ
TN_GATE = 256
TF = 1024
BF16_SUBLANES = 16
TN_ADA = 1024
ADA_ROWS = 136
VMEM_LIMIT = 60 * 1024 * 1024


def _dot(a, b):
    return jnp.dot(a, b, preferred_element_type=F32)


def _rms_scale(x, g):
    return x * lax.rsqrt(jnp.mean(x * x, axis=-1, keepdims=True) + EPS) * g


def _modulated_norm(x_ref, g_ref, sc_ref, sh_ref, h_ref, tm):
    def body(c, carry):
        r0 = pl.multiple_of(c * ROWS, ROWS)
        x = x_ref[pl.ds(r0, ROWS), :]
        r = lax.rsqrt(jnp.mean(x * x, axis=-1, keepdims=True) + EPS)
        h = x_ref[pl.ds(r0, ROWS), :] * r * (g_ref[...] * (1.0 + sc_ref[...])) + sh_ref[...]
        h_ref[pl.ds(r0, ROWS), :] = h.astype(BF16)
        return carry
    lax.fori_loop(0, tm // ROWS, body, 0)


def _ada_kernel(c_ref, w_ref, b_ref, o_ref):
    a = jax.nn.silu(c_ref[...]).astype(BF16)
    o_ref[...] = _dot(a, w_ref[...].astype(BF16)) + b_ref[...]


def _ada_call(c_all, w_ada, b_ada):
    n = N_MOD * D_MODEL
    return pl.pallas_call(
        _ada_kernel,
        grid=(DEPTH, n // TN_ADA),
        in_specs=[
            pl.BlockSpec((ADA_ROWS, D_MODEL), lambda l, j: (0, 0)),
            pl.BlockSpec((None, D_MODEL, TN_ADA), lambda l, j: (l, 0, j)),
            pl.BlockSpec((None, 1, TN_ADA), lambda l, j: (l, 0, j)),
        ],
        out_specs=pl.BlockSpec((None, ADA_ROWS, TN_ADA), lambda l, j: (l, 0, j)),
        out_shape=jax.ShapeDtypeStruct((DEPTH, ADA_ROWS, n), F32),
        compiler_params=pltpu.CompilerParams(
            dimension_semantics=("arbitrary", "arbitrary"), vmem_limit_bytes=VMEM_LIMIT),
        name="ada",
    )(c_all, w_ada, b_ada.reshape(DEPTH, 1, n))


def _mix_prompt_kernel(x_ref, n1_ref, sh_ref, sc_ref, wa_ref, wb_ref, wc_ref, wgrp_ref, pscale_ref,
                       wconv_ref, sgun_ref, wsgu_ref, bsgu_ref,
                       br_ref, ptail_ref, ztail_ref, vtail_ref,
                       hbuf, t0, t1, pbuf, zbuf, zhalo, *, tm, tiles_per_seq):
    i = pl.program_id(0)
    j = pl.program_id(1)
    tile_in_seq = i % tiles_per_seq
    first_tile = tile_in_seq == 0
    n_slabs = tm // ROWS

    @pl.when(j == 0)
    def _():
        _modulated_norm(x_ref, n1_ref, sc_ref, sh_ref, hbuf, tm)

    @pl.when(j == 0)
    def _pool():
        @pl.when(first_tile)
        def _():
            pbuf[0:POOL_HALO, :] = jnp.zeros((POOL_HALO, CB), F32)

        @pl.when(jnp.logical_not(first_tile))
        def _():
            pbuf[0:POOL_HALO, :] = pbuf[tm:tm + POOL_HALO, :]

        pbuf[POOL_HALO:POOL_HALO + tm, :] = _dot(hbuf[...], wa_ref[...])
        ptail_ref[...] = pbuf[tm:tm + POOL_HALO, :]
        pos0 = tile_in_seq * tm

        def body(c, carry):
            r0 = pl.multiple_of(c * ROWS, ROWS)
            pos = pos0 + r0 + lax.broadcasted_iota(jnp.int32, (ROWS, 1), 0)
            for g, w in enumerate(POOL_WINDOWS):
                cols = slice(g * GROUP, (g + 1) * GROUP)
                ext = pbuf[pl.ds(r0, ROWS + POOL_HALO), cols]
                s = ext
                k = 1
                while k < w:
                    s = s + pltpu.roll(s, k, axis=0)
                    k *= 2
                cnt = jnp.minimum(pos + 1, w).astype(F32)
                d = s[POOL_HALO:, :] / cnt - ext[POOL_HALO:, :]
                y = _dot(d.astype(BF16), wgrp_ref[g].astype(BF16)) * pscale_ref[:, cols]
                br_ref[pl.ds(r0, ROWS), cols] = y.astype(BF16)
            return carry
        lax.fori_loop(0, n_slabs, body, 0)

    @pl.when(jnp.logical_and(j >= 1, j <= 2))
    def _conv():
        cb = j - 1

        @pl.when(first_tile)
        def _():
            zbuf[0:CONV_HALO, :] = jnp.zeros((CONV_HALO, CB), F32)

        @pl.when(jnp.logical_not(first_tile))
        def _():
            zbuf[0:CONV_HALO, :] = zhalo[cb]

        h = hbuf[...]
        zbuf[CONV_HALO:CONV_HALO + tm, :] = _dot(h, wc_ref[...]) * _dot(h, wa_ref[...])
        t0[...] = _dot(h, wb_ref[...])
        tail = zbuf[tm:tm + CONV_HALO, :]
        zhalo[cb] = tail
        ztail_ref[...] = tail
        w0 = wconv_ref[0:1, :]
        w1 = wconv_ref[1:2, :]
        w2 = wconv_ref[2:3, :]

        def body(c, carry):
            r0 = pl.multiple_of(c * ROWS, ROWS)
            ext = zbuf[pl.ds(r0, ROWS + CONV_HALO), :]
            conv = (w0 * pltpu.roll(ext, 2, axis=0)[CONV_HALO:, :]
                    + w1 * pltpu.roll(ext, 1, axis=0)[CONV_HALO:, :]
                    + w2 * ext[CONV_HALO:, :])
            br_ref[pl.ds(r0, ROWS), :] = (t0[pl.ds(r0, ROWS), :] * conv).astype(BF16)
            return carry
        lax.fori_loop(0, n_slabs, body, 0)

    @pl.when(j == 3)
    def _sgu():
        h = hbuf[...]
        t0[...] = _dot(h, wa_ref[...])
        t1[...] = _dot(h, wb_ref[...])
        row = lax.broadcasted_iota(jnp.int32, (CHUNK, CHUNK), 0)
        col = lax.broadcasted_iota(jnp.int32, (CHUNK, CHUNK), 1)
        causal = row >= col

        def body(c, carry):
            r0 = pl.multiple_of(c * ROWS, ROWS)
            vn = _rms_scale(jax.nn.gelu(t1[pl.ds(r0, ROWS), :]), sgun_ref[...])

            @pl.when(c == n_slabs - 1)
            def _():
                vtail_ref[...] = vn

            for g in range(SGU_WIDTH // GROUP):
                cols = slice(g * GROUP, (g + 1) * GROUP)
                wt = jnp.where(causal, wsgu_ref[g], 0.0).astype(BF16)
                sg = _dot(wt, vn[:, cols].astype(BF16)) + bsgu_ref[:, g:g + 1]
                out = jax.nn.gelu(t0[pl.ds(r0, ROWS), cols]) * sg
                br_ref[pl.ds(r0, ROWS), cols] = out.astype(BF16)
            return carry
        lax.fori_loop(0, n_slabs, body, 0)


def _win_index_a(j):
    return jnp.where(j == 3, 7, j)


def _win_index_b(j):
    return jnp.where(j == 3, 8, jnp.maximum(j, 1) + 2)


def _win_index_c(j):
    return jnp.clip(j, 1, 2) + 4


def _mix_prompt_call(l, x, modp, p):
    tm = TM_PROMPT
    tps = SEQ // tm
    n_tiles = M_PROMPT // tm

    def mod_spec(k):
        return pl.BlockSpec((None, None, None, 1, D_MODEL), lambda i, j: (l, i // tps, k, 0, 0))

    full3 = lambda shape: pl.BlockSpec((None,) + shape, lambda i, j: (l,) + (0,) * len(shape))
    kern = functools.partial(_mix_prompt_kernel, tm=tm, tiles_per_seq=tps)
    return pl.pallas_call(
        kern,
        grid=(n_tiles, 4),
        in_specs=[
            pl.BlockSpec((tm, D_MODEL), lambda i, j: (i, 0)),
            full3((1, D_MODEL)),
            mod_spec(0), mod_spec(1),
            pl.BlockSpec((D_MODEL, CB), lambda i, j: (0, _win_index_a(j))),
            pl.BlockSpec((D_MODEL, CB), lambda i, j: (0, _win_index_b(j))),
            pl.BlockSpec((D_MODEL, CB), lambda i, j: (0, _win_index_c(j))),
            full3((4, GROUP, GROUP)),
            full3((1, POOL_WIDTH)),
            pl.BlockSpec((None, CONV_K, CB), lambda i, j: (l, 0, jnp.clip(j - 1, 0, 1))),
            full3((1, SGU_WIDTH)),
            full3((4, CHUNK, CHUNK)),
            full3((CHUNK, 4)),
        ],
        out_specs=[
            pl.BlockSpec((tm, CB), lambda i, j: (i, j)),
            pl.BlockSpec((None, POOL_HALO, CB), lambda i, j: (i, 0, 0)),
            pl.BlockSpec((None, CONV_HALO, CB), lambda i, j: (i, 0, jnp.clip(j - 1, 0, 1))),
            pl.BlockSpec((CHUNK, SGU_WIDTH), lambda i, j: (i, 0)),
        ],
        out_shape=[
            jax.ShapeDtypeStruct((M_PROMPT, D_MODEL), BF16),
            jax.ShapeDtypeStruct((n_tiles, POOL_HALO, POOL_WIDTH), F32),
            jax.ShapeDtypeStruct((n_tiles, CONV_HALO, CONV_WIDTH), F32),
            jax.ShapeDtypeStruct((n_tiles * CHUNK, SGU_WIDTH), F32),
        ],
        scratch_shapes=[
            pltpu.VMEM((tm, D_MODEL), BF16),
            pltpu.VMEM((tm, CB), F32),
            pltpu.VMEM((tm, CB), F32),
            pltpu.VMEM((tm + POOL_HALO, CB), F32),
            pltpu.VMEM((tm + CONV_HALO, CB), F32),
            pltpu.VMEM((CONV_WIDTH // CB, CONV_HALO, CB), F32),
        ],
        compiler_params=pltpu.CompilerParams(
            dimension_semantics=("arbitrary", "arbitrary"), vmem_limit_bytes=VMEM_LIMIT),
        name="mix_prompt",
    )(x, p["norm1"], modp, modp, p["w_in"], p["w_in"], p["w_in"], p["w_pool_grp"], p["pool_scale"],
      p["w_conv"], p["sgu_norm"], p["w_sgu"], p["b_sgu_t"])


def _mix_sample_kernel(x_ref, n1_ref, sh_ref, sc_ref, wa_ref, wb_ref, wc_ref, wgrp_ref, pscale_ref,
                       wconv_ref, sgun_ref, wv_ref, bv_ref, spool_ref, sconv_ref,
                       br_ref, pool_ref, conv_ref, v_ref,
                       hbuf, t0, t1):
    j = pl.program_id(0)
    nb = DEC_BATCH
    T = DEC_SEQ

    def slab(t):
        return slice(t * nb, (t + 1) * nb)

    @pl.when(j == 0)
    def _():
        _modulated_norm(x_ref, n1_ref, sc_ref, sh_ref, hbuf, T * nb)

    @pl.when(j == 0)
    def _pool():
        t0[...] = _dot(hbuf[...], wa_ref[...])
        keep = POOL_BUF - T
        pool_ref[0:keep * nb, :] = spool_ref[T * nb:POOL_BUF * nb, :]
        pool_ref[keep * nb:POOL_BUF * nb, :] = t0[...]

        def full(s, cols):
            if s < POOL_BUF:
                return spool_ref[slab(s), cols]
            return t0[slab(s - POOL_BUF), cols]

        for t in range(T):
            for g, w in enumerate(POOL_WINDOWS):
                cols = slice(g * GROUP, (g + 1) * GROUP)
                s = full(POOL_BUF + t, cols)
                for k in range(1, w):
                    s = s + full(POOL_BUF + t - k, cols)
                cnt = float(min(PAST_LEN + t + 1, w))
                d = s / cnt - t0[slab(t), cols]
                y = _dot(d.astype(BF16), wgrp_ref[g].astype(BF16)) * pscale_ref[:, cols]
                br_ref[slab(t), cols] = y.astype(BF16)

    @pl.when(jnp.logical_and(j >= 1, j <= 2))
    def _conv():
        h = hbuf[...]
        t1[...] = _dot(h, wc_ref[...]) * _dot(h, wa_ref[...])
        t0[...] = _dot(h, wb_ref[...])

        def full(s):
            if s < CONV_BUF:
                return sconv_ref[slab(s), :]
            return t1[slab(s - CONV_BUF), :]

        for s in range(CONV_BUF):
            conv_ref[slab(s), :] = full(T + s)
        for t in range(T):
            conv = (wconv_ref[0:1, :] * full(t) + wconv_ref[1:2, :] * full(t + 1)
                    + wconv_ref[2:3, :] * full(t + 2))
            br_ref[slab(t), :] = (t0[slab(t), :] * conv).astype(BF16)

    @pl.when(j == 3)
    def _sgu():
        h = hbuf[...]
        t0[...] = _dot(h, wa_ref[...])
        t1[...] = _dot(h, wb_ref[...])
        for t in range(T):
            v_ref[slab(t), :] = _rms_scale(jax.nn.gelu(t1[slab(t), :]), sgun_ref[...])
        for t in range(T):
            sg = bv_ref[t:t + 1, :]
            for k in range(t + 1):
                sg = sg + wv_ref[t * T + k:t * T + k + 1, :] * v_ref[slab(k), :]
            br_ref[slab(t), :] = (jax.nn.gelu(t0[slab(t), :]) * sg).astype(BF16)


def _mix_sample_call(l, x, mod, p, state_pool_tm, state_conv_tm):
    tm = M_SAMPLE

    def mod_spec(k):
        return pl.BlockSpec((None, DEC_BATCH, D_MODEL), lambda j: (l, 0, k))

    full3 = lambda shape: pl.BlockSpec((None,) + shape, lambda j: (l,) + (0,) * len(shape))
    cbi = lambda j: jnp.clip(j - 1, 0, 1)
    return pl.pallas_call(
        _mix_sample_kernel,
        grid=(4,),
        in_specs=[
            pl.BlockSpec((tm, D_MODEL), lambda j: (0, 0)),
            full3((1, D_MODEL)),
            mod_spec(0), mod_spec(1),
            pl.BlockSpec((D_MODEL, CB), lambda j: (0, _win_index_a(j))),
            pl.BlockSpec((D_MODEL, CB), lambda j: (0, _win_index_b(j))),
            pl.BlockSpec((D_MODEL, CB), lambda j: (0, _win_index_c(j))),
            full3((4, GROUP, GROUP)),
            full3((1, POOL_WIDTH)),
            pl.BlockSpec((None, CONV_K, CB), lambda j: (l, 0, cbi(j))),
            full3((1, SGU_WIDTH)),
            full3((DEC_SEQ * DEC_SEQ, SGU_WIDTH)),
            full3((DEC_SEQ, SGU_WIDTH)),
            full3((POOL_BUF * DEC_BATCH, POOL_WIDTH)),
            pl.BlockSpec((None, CONV_BUF * DEC_BATCH, CB), lambda j: (l, 0, cbi(j))),
        ],
        out_specs=[
            pl.BlockSpec((tm, CB), lambda j: (0, j)),
            pl.BlockSpec((POOL_BUF * DEC_BATCH, POOL_WIDTH), lambda j: (0, 0)),
            pl.BlockSpec((CONV_BUF * DEC_BATCH, CB), lambda j: (0, cbi(j))),
            pl.BlockSpec((tm, SGU_WIDTH), lambda j: (0, 0)),
        ],
        out_shape=[
            jax.ShapeDtypeStruct((tm, D_MODEL), BF16),
            jax.ShapeDtypeStruct((POOL_BUF * DEC_BATCH, POOL_WIDTH), F32),
            jax.ShapeDtypeStruct((CONV_BUF * DEC_BATCH, CONV_WIDTH), F32),
            jax.ShapeDtypeStruct((tm, SGU_WIDTH), F32),
        ],
        scratch_shapes=[
            pltpu.VMEM((tm, D_MODEL), BF16),
            pltpu.VMEM((tm, CB), F32),
            pltpu.VMEM((tm, CB), F32),
        ],
        compiler_params=pltpu.CompilerParams(
            dimension_semantics=("arbitrary",), vmem_limit_bytes=VMEM_LIMIT),
        name="mix_sample",
    )(x, p["norm1"], mod, mod, p["w_in"], p["w_in"], p["w_in"], p["w_pool_grp"], p["pool_scale"],
      p["w_conv"], p["sgu_norm"], p["w_sgu_v"], p["b_sgu_v"], state_pool_tm, state_conv_tm)


def _cast_specs(l_next, w, n_steps, step_of):
    _, rows, cols = w.shape
    rows_per = max(BF16_SUBLANES, rows // n_steps)
    reps = n_steps * rows_per // rows
    in_spec = pl.BlockSpec((None, rows_per, cols), lambda i, j: (l_next, step_of(i, j) // reps, 0))
    out_spec = pl.BlockSpec((rows_per, cols), lambda i, j: (step_of(i, j) // reps, 0))
    return in_spec, out_spec, jax.ShapeDtypeStruct((rows, cols), BF16)


def _cast_slices(src_refs, dst_refs):
    for src, dst in zip(src_refs, dst_refs):
        dst[...] = src[...].astype(BF16)


def _gate_kernel(*refs, tm, n1, n_cast):
    (x_ref, br_ref, n1_ref, sh_ref, sc_ref, g_ref, wg0_ref, wg1_ref, wg2_ref,
     wbp_ref, wbc_ref, wbs_ref, wout_ref) = refs[:13]
    cast_src = refs[13:13 + n_cast]
    o_ref = refs[13 + n_cast]
    cast_dst = refs[14 + n_cast:14 + 2 * n_cast]
    hbuf = refs[14 + 2 * n_cast]
    j = pl.program_id(1)
    half = tm // 2

    @pl.when(j == 0)
    def _():
        _modulated_norm(x_ref, n1_ref, sc_ref, sh_ref, hbuf, tm)
        o_ref[...] = jnp.zeros((tm, D_MODEL), F32)

    _cast_slices(cast_src, cast_dst)
    for hh in range(2):
        rows = slice(hh * half, (hh + 1) * half)
        h = hbuf[rows, :]
        m = (jax.nn.sigmoid(_dot(h, wg0_ref[...]))
             * _dot(br_ref[rows, 0:POOL_WIDTH], wbp_ref[...]))
        m = m + (jax.nn.sigmoid(_dot(h, wg1_ref[...]))
                 * _dot(br_ref[rows, POOL_WIDTH:POOL_WIDTH + CONV_WIDTH], wbc_ref[...]))
        m = m + (jax.nn.sigmoid(_dot(h, wg2_ref[...]))
                 * _dot(br_ref[rows, POOL_WIDTH + CONV_WIDTH:], wbs_ref[...]))
        o_ref[rows, :] += _dot(m.astype(BF16), wout_ref[...])

    @pl.when(j == n1 - 1)
    def _():
        def body(c, carry):
            r0 = pl.multiple_of(c * ROWS, ROWS)
            o_ref[pl.ds(r0, ROWS), :] = (x_ref[pl.ds(r0, ROWS), :]
                                         + g_ref[...] * o_ref[pl.ds(r0, ROWS), :])
            return carry
        lax.fori_loop(0, tm // ROWS, body, 0)


def _gate_call(l, x, br, p, grp, cast_from):
    tm, n_tiles, mod_arr, mod_spec = grp["tm"], grp["n_tiles"], grp["mod"], grp["mod_spec"]
    n1 = D_MODEL // TN_GATE
    gate0 = OFF_GATE // TN_GATE

    def gate_spec(k):
        return pl.BlockSpec((D_MODEL, TN_GATE), lambda i, j: (0, gate0 + k * n1 + j))

    def branch_spec(width):
        return pl.BlockSpec((width, TN_GATE), lambda i, j: (0, j))

    casts = []
    if cast_from is not None:
        casts = [_cast_specs(l + 1, w, n_tiles * n1, lambda i, j: i * n1 + j) for w in cast_from]
    kern = functools.partial(_gate_kernel, tm=tm, n1=n1, n_cast=len(casts))
    outs = pl.pallas_call(
        kern,
        grid=(n_tiles, n1),
        in_specs=[
            pl.BlockSpec((tm, D_MODEL), lambda i, j: (i, 0), pipeline_mode=pl.Buffered(1)),
            pl.BlockSpec((tm, D_MODEL), lambda i, j: (i, 0), pipeline_mode=pl.Buffered(1)),
            pl.BlockSpec((None, 1, D_MODEL), lambda i, j: (l, 0, 0)),
            mod_spec(l, 0), mod_spec(l, 1), mod_spec(l, 2),
            gate_spec(0), gate_spec(1), gate_spec(2),
            branch_spec(POOL_WIDTH), branch_spec(CONV_WIDTH), branch_spec(SGU_WIDTH),
            pl.BlockSpec((TN_GATE, D_MODEL), lambda i, j: (j, 0)),
        ] + [c[0] for c in casts],
        out_specs=[pl.BlockSpec((tm, D_MODEL), lambda i, j: (i, 0))] + [c[1] for c in casts],
        out_shape=[jax.ShapeDtypeStruct(x.shape, F32)] + [c[2] for c in casts],
        scratch_shapes=[pltpu.VMEM((tm, D_MODEL), BF16)],
        compiler_params=pltpu.CompilerParams(
            dimension_semantics=("arbitrary", "arbitrary"), vmem_limit_bytes=VMEM_LIMIT),
        name="gate_" + grp["name"],
    )(x, br, p["norm1"], mod_arr, mod_arr, mod_arr, p["w_in"], p["w_in"], p["w_in"],
      p["w_br_pool"], p["w_br_conv"], p["w_br_sgu"], p["w_out"], *(cast_from or []))
    return outs[0], outs[1:]


def _ffn_kernel(*refs, tm, nf, final_norm, n_cast):
    x_ref, n2_ref, sh_ref, sc_ref, g_ref, fn_ref, w1_ref, w2_ref = refs[:8]
    cast_src = refs[8:8 + n_cast]
    o_ref = refs[8 + n_cast]
    cast_dst = refs[9 + n_cast:9 + 2 * n_cast]
    hbuf = refs[9 + 2 * n_cast]
    f = pl.program_id(1)
    half = tm // 2

    @pl.when(f == 0)
    def _():
        _modulated_norm(x_ref, n2_ref, sc_ref, sh_ref, hbuf, tm)
        o_ref[...] = jnp.zeros((tm, D_MODEL), F32)

    _cast_slices(cast_src, cast_dst)

    for hh in range(2):
        rows = slice(hh * half, (hh + 1) * half)
        a = jnp.square(jax.nn.relu(_dot(hbuf[rows, :], w1_ref[...]))).astype(BF16)
        o_ref[rows, :] += _dot(a, w2_ref[...])

    @pl.when(f == nf - 1)
    def _():
        def body(c, carry):
            r0 = pl.multiple_of(c * ROWS, ROWS)
            y = x_ref[pl.ds(r0, ROWS), :] + g_ref[...] * o_ref[pl.ds(r0, ROWS), :]
            if final_norm:
                y = _rms_scale(y, fn_ref[...])
            o_ref[pl.ds(r0, ROWS), :] = y
            return carry
        lax.fori_loop(0, tm // ROWS, body, 0)


def _ffn_call(l, x, p, grp, final_norm, cast_from):
    tm, n_tiles, mod_arr, mod_spec = grp["tm"], grp["n_tiles"], grp["mod"], grp["mod_spec"]
    nf = D_FF // TF
    casts = []
    if cast_from is not None:
        casts = [_cast_specs(l + 1, w, n_tiles * nf, lambda i, f: i * nf + f) for w in cast_from]
    kern = functools.partial(_ffn_kernel, tm=tm, nf=nf, final_norm=final_norm, n_cast=len(casts))
    outs = pl.pallas_call(
        kern,
        grid=(n_tiles, nf),
        in_specs=[
            pl.BlockSpec((tm, D_MODEL), lambda i, f: (i, 0)),
            pl.BlockSpec((None, 1, D_MODEL), lambda i, f: (l, 0, 0)),
            mod_spec(l, 3), mod_spec(l, 4), mod_spec(l, 5),
            pl.BlockSpec((1, D_MODEL), lambda i, f: (0, 0)),
            pl.BlockSpec((D_MODEL, TF), lambda i, f: (0, f)),
            pl.BlockSpec((TF, D_MODEL), lambda i, f: (f, 0)),
        ] + [c[0] for c in casts],
        out_specs=[pl.BlockSpec((tm, D_MODEL), lambda i, f: (i, 0))] + [c[1] for c in casts],
        out_shape=[jax.ShapeDtypeStruct(x.shape, F32)] + [c[2] for c in casts],
        scratch_shapes=[pltpu.VMEM((tm, D_MODEL), BF16)],
        compiler_params=pltpu.CompilerParams(
            dimension_semantics=("arbitrary", "arbitrary"), vmem_limit_bytes=VMEM_LIMIT),
        name="ffn_" + grp["name"],
    )(x, p["norm2"], mod_arr, mod_arr, mod_arr, p["final_norm"], p["w_ff1"], p["w_ff2"],
      *(cast_from or []))
    return outs[0], outs[1:]


_GATE_CAST = ("w_in", "w_br_pool", "w_br_conv", "w_br_sgu", "w_out", "w_ff2")
_FFN_CAST = ("w_ff1",)


def kernel(x_prompt, x_sample, state_pool, state_conv, c_prompt, c_sample, norm1, norm2, w_ada, b_ada,
           w_in, w_pool_grp, pool_scale, w_conv, sgu_norm, w_sgu, b_sgu, w_br_pool, w_br_conv,
           w_br_sgu, w_out, w_ff1, w_ff2, final_norm):
    T, nb = DEC_SEQ, DEC_BATCH
    tps = SEQ // TM_PROMPT

    c_all = jnp.concatenate(
        [c_sample, c_prompt, jnp.zeros((ADA_ROWS - nb - BATCH, D_MODEL), F32)], axis=0)
    mod = _ada_call(c_all, w_ada, b_ada)
    modp = mod[:, nb:nb + BATCH].reshape(DEPTH, BATCH, N_MOD, 1, D_MODEL)

    small = {
        "norm1": norm1.reshape(DEPTH, 1, D_MODEL),
        "norm2": norm2.reshape(DEPTH, 1, D_MODEL),
        "final_norm": final_norm.reshape(1, D_MODEL),
        "w_pool_grp": w_pool_grp,
        "pool_scale": pool_scale.reshape(DEPTH, 1, POOL_WIDTH),
        "w_conv": w_conv,
        "sgu_norm": sgu_norm.reshape(DEPTH, 1, SGU_WIDTH),
        "w_sgu": w_sgu,
        "b_sgu_t": jnp.transpose(b_sgu, (0, 2, 1)),
        "w_sgu_v": jnp.repeat(
            jnp.transpose(w_sgu[:, :, :T, :T], (0, 2, 3, 1)).reshape(DEPTH, T * T, 4), GROUP, axis=-1),
        "b_sgu_v": jnp.repeat(jnp.transpose(b_sgu[:, :, :T], (0, 2, 1)), GROUP, axis=-1),
    }
    big = {"w_in": w_in, "w_br_pool": w_br_pool, "w_br_conv": w_br_conv, "w_br_sgu": w_br_sgu,
           "w_out": w_out, "w_ff1": w_ff1, "w_ff2": w_ff2}
    wl = {k: v[0].astype(BF16) for k, v in big.items()}

    grp_p = {
        "name": "prompt", "tm": TM_PROMPT, "n_tiles": M_PROMPT // TM_PROMPT, "mod": modp,
        "mod_spec": lambda l, k: pl.BlockSpec(
            (None, None, None, 1, D_MODEL), lambda i, j: (l, i // tps, k, 0, 0)),
    }
    grp_s = {
        "name": "sample", "tm": M_SAMPLE, "n_tiles": 1, "mod": mod,
        "mod_spec": lambda l, k: pl.BlockSpec((None, nb, D_MODEL), lambda i, j: (l, 0, k)),
    }

    xp = x_prompt.reshape(M_PROMPT, D_MODEL)
    xs = jnp.transpose(x_sample, (1, 0, 2)).reshape(M_SAMPLE, D_MODEL)
    spool_tm = jnp.transpose(state_pool, (0, 2, 1, 3)).reshape(DEPTH, POOL_BUF * nb, POOL_WIDTH)
    sconv_tm = jnp.transpose(state_conv, (0, 2, 1, 3)).reshape(DEPTH, CONV_BUF * nb, CONV_WIDTH)

    pool_p, conv_p, v_p, pool_s, conv_s, v_s = [], [], [], [], [], []
    last_tile = slice(tps - 1, None, tps)
    for l in range(DEPTH):
        final = l == DEPTH - 1
        p = dict(small, **wl)
        gate_cast = None if final else [big[k] for k in _GATE_CAST]
        ffn_cast = None if final else [big[k] for k in _FFN_CAST]

        br, ptail, ztail, vtail = _mix_prompt_call(l, xp, modp, p)
        xp, gate_next = _gate_call(l, xp, br, p, grp_p, gate_cast)
        xp, ffn_next = _ffn_call(l, xp, p, grp_p, final, ffn_cast)
        pool_p.append(ptail[last_tile, POOL_HALO - POOL_BUF:, :])
        conv_p.append(ztail[last_tile, CONV_HALO - CONV_BUF:, :])
        v_p.append(vtail.reshape(-1, CHUNK, SGU_WIDTH)[last_tile])

        br, npool, nconv, nv = _mix_sample_call(l, xs, mod, p, spool_tm, sconv_tm)
        xs, _ = _gate_call(l, xs, br, p, grp_s, None)
        xs, _ = _ffn_call(l, xs, p, grp_s, final, None)
        pool_s.append(jnp.transpose(npool.reshape(POOL_BUF, nb, POOL_WIDTH), (1, 0, 2)))
        conv_s.append(jnp.transpose(nconv.reshape(CONV_BUF, nb, CONV_WIDTH), (1, 0, 2)))
        v_s.append(jnp.transpose(nv.reshape(T, nb, SGU_WIDTH), (1, 0, 2)))
        if not final:
            wl = dict(zip(_GATE_CAST + _FFN_CAST, list(gate_next) + list(ffn_next)))

    y_prompt = xp.reshape(BATCH, SEQ, D_MODEL)
    y_sample = jnp.transpose(xs.reshape(T, nb, D_MODEL), (1, 0, 2))
    return (y_prompt, y_sample, jnp.stack(pool_p), jnp.stack(conv_p), jnp.stack(v_p),
            jnp.stack(pool_s), jnp.stack(conv_s), jnp.stack(v_s))
```

```python
import functools

import jax
import jax.numpy as jnp
from jax import lax
from jax.experimental import pallas as pl
from jax.experimental.pallas import tpu as pltpu

F32 = jnp.float32
BF16 = jnp.bfloat16

D_MODEL = 2048
DEPTH = 4
BATCH = 4
SEQ = 2048
DEC_BATCH = 128
DEC_SEQ = 4
PAST_LEN = 16384
POOL_WINDOWS = (2, 4, 8, 16)
GROUP = 128
POOL_WIDTH = 512
POOL_BUF = 15
CONV_WIDTH = 1024
CONV_K = 3
CONV_BUF = 2
CHUNK = 128
SGU_WIDTH = 512
N_MOD = 6
D_FF = 4 * D_MODEL
EPS = 1e-6
OFF_GATE = POOL_WIDTH + 3 * CONV_WIDTH + 2 * SGU_WIDTH
N_IN = OFF_GATE + 3 * D_MODEL

ROWS = 128
CB = 512
POOL_HALO = 16
CONV_HALO = 8
TM_PROMPT = 1024
M_PROMPT = BATCH * SEQ
M_SAMPLE = DEC_BATCH * DEC_SEQ
TN_GATE = 256
TF = 1024
BF16_SUBLANES = 16
TN_ADA = 1024
ADA_ROWS = 136
VMEM_LIMIT = 60 * 1024 * 1024


def _dot(a, b):
    return jnp.dot(a, b, preferred_element_type=F32)


def _rms_scale(x, g):
    return x * lax.rsqrt(jnp.mean(x * x, axis=-1, keepdims=True) + EPS) * g


def _modulated_norm(x_ref, g_ref, sc_ref, sh_ref, h_ref, tm):
    def body(c, carry):
        r0 = pl.multiple_of(c * ROWS, ROWS)
        x = x_ref[pl.ds(r0, ROWS), :]
        r = lax.rsqrt(jnp.mean(x * x, axis=-1, keepdims=True) + EPS)
        h = x_ref[pl.ds(r0, ROWS), :] * r * (g_ref[...] * (1.0 + sc_ref[...])) + sh_ref[...]
        h_ref[pl.ds(r0, ROWS), :] = h.astype(BF16)
        return carry
    lax.fori_loop(0, tm // ROWS, body, 0)


def _ada_kernel(c_ref, w_ref, b_ref, o_ref):
    a = jax.nn.silu(c_ref[...]).astype(BF16)
    o_ref[...] = _dot(a, w_ref[...].astype(BF16)) + b_ref[...]


def _ada_call(c_all, w_ada, b_ada):
    n = N_MOD * D_MODEL
    return pl.pallas_call(
        _ada_kernel,
        grid=(DEPTH, n // TN_ADA),
        in_specs=[
            pl.BlockSpec((ADA_ROWS, D_MODEL), lambda l, j: (0, 0)),
            pl.BlockSpec((None, D_MODEL, TN_ADA), lambda l, j: (l, 0, j)),
            pl.BlockSpec((None, 1, TN_ADA), lambda l, j: (l, 0, j)),
        ],
        out_specs=pl.BlockSpec((None, ADA_ROWS, TN_ADA), lambda l, j: (l, 0, j)),
        out_shape=jax.ShapeDtypeStruct((DEPTH, ADA_ROWS, n), F32),
        compiler_params=pltpu.CompilerParams(
            dimension_semantics=("arbitrary", "arbitrary"), vmem_limit_bytes=VMEM_LIMIT),
        name="ada",
    )(c_all, w_ada, b_ada.reshape(DEPTH, 1, n))


def _mix_prompt_kernel(x_ref, n1_ref, sh_ref, sc_ref, wa_ref, wb_ref, wc_ref, wgrp_ref, pscale_ref,
                       wconv_ref, sgun_ref, wsgu_ref, bsgu_ref,
                       br_ref, ptail_ref, ztail_ref, vtail_ref,
                       hbuf, t0, t1, pbuf, zbuf, zhalo, *, tm, tiles_per_seq):
    i = pl.program_id(0)
    j = pl.program_id(1)
    tile_in_seq = i % tiles_per_seq
    first_tile = tile_in_seq == 0
    n_slabs = tm // ROWS

    @pl.when(j == 0)
    def _():
        _modulated_norm(x_ref, n1_ref, sc_ref, sh_ref, hbuf, tm)

    @pl.when(j == 0)
    def _pool():
        @pl.when(first_tile)
        def _():
            pbuf[0:POOL_HALO, :] = jnp.zeros((POOL_HALO, CB), F32)

        @pl.when(jnp.logical_not(first_tile))
        def _():
            pbuf[0:POOL_HALO, :] = pbuf[tm:tm + POOL_HALO, :]

        pbuf[POOL_HALO:POOL_HALO + tm, :] = _dot(hbuf[...], wa_ref[...])
        ptail_ref[...] = pbuf[tm:tm + POOL_HALO, :]
        pos0 = tile_in_seq * tm

        def body(c, carry):
            r0 = pl.multiple_of(c * ROWS, ROWS)
            pos = pos0 + r0 + lax.broadcasted_iota(jnp.int32, (ROWS, 1), 0)
            for g, w in enumerate(POOL_WINDOWS):
                cols = slice(g * GROUP, (g + 1) * GROUP)
                ext = pbuf[pl.ds(r0, ROWS + POOL_HALO), cols]
                s = ext
                k = 1
                while k < w:
                    s = s + pltpu.roll(s, k, axis=0)
                    k *= 2
                cnt = jnp.minimum(pos + 1, w).astype(F32)
                d = s[POOL_HALO:, :] / cnt - ext[POOL_HALO:, :]
                y = _dot(d.astype(BF16), wgrp_ref[g].astype(BF16)) * pscale_ref[:, cols]
                br_ref[pl.ds(r0, ROWS), cols] = y.astype(BF16)
            return carry
        lax.fori_loop(0, n_slabs, body, 0)

    @pl.when(jnp.logical_and(j >= 1, j <= 2))
    def _conv():
        cb = j - 1

        @pl.when(first_tile)
        def _():
            zbuf[0:CONV_HALO, :] = jnp.zeros((CONV_HALO, CB), F32)

        @pl.when(jnp.logical_not(first_tile))
        def _():
            zbuf[0:CONV_HALO, :] = zhalo[cb]

        h = hbuf[...]
        zbuf[CONV_HALO:CONV_HALO + tm, :] = _dot(h, wc_ref[...]) * _dot(h, wa_ref[...])
        t0[...] = _dot(h, wb_ref[...])
        tail = zbuf[tm:tm + CONV_HALO, :]
        zhalo[cb] = tail
        ztail_ref[...] = tail
        w0 = wconv_ref[0:1, :]
        w1 = wconv_ref[1:2, :]
        w2 = wconv_ref[2:3, :]

        def body(c, carry):
            r0 = pl.multiple_of(c * ROWS, ROWS)
            ext = zbuf[pl.ds(r0, ROWS + CONV_HALO), :]
            conv = (w0 * pltpu.roll(ext, 2, axis=0)[CONV_HALO:, :]
                    + w1 * pltpu.roll(ext, 1, axis=0)[CONV_HALO:, :]
                    + w2 * ext[CONV_HALO:, :])
            br_ref[pl.ds(r0, ROWS), :] = (t0[pl.ds(r0, ROWS), :] * conv).astype(BF16)
            return carry
        lax.fori_loop(0, n_slabs, body, 0)

    @pl.when(j == 3)
    def _sgu():
        h = hbuf[...]
        t0[...] = _dot(h, wa_ref[...])
        t1[...] = _dot(h, wb_ref[...])
        row = lax.broadcasted_iota(jnp.int32, (CHUNK, CHUNK), 0)
        col = lax.broadcasted_iota(jnp.int32, (CHUNK, CHUNK), 1)
        causal = row >= col

        def body(c, carry):
            r0 = pl.multiple_of(c * ROWS, ROWS)
            vn = _rms_scale(jax.nn.gelu(t1[pl.ds(r0, ROWS), :]), sgun_ref[...])

            @pl.when(c == n_slabs - 1)
            def _():
                vtail_ref[...] = vn

            for g in range(SGU_WIDTH // GROUP):
                cols = slice(g * GROUP, (g + 1) * GROUP)
                wt = jnp.where(causal, wsgu_ref[g], 0.0).astype(BF16)
                sg = _dot(wt, vn[:, cols].astype(BF16)) + bsgu_ref[:, g:g + 1]
                out = jax.nn.gelu(t0[pl.ds(r0, ROWS), cols]) * sg
                br_ref[pl.ds(r0, ROWS), cols] = out.astype(BF16)
            return carry
        lax.fori_loop(0, n_slabs, body, 0)


def _win_index_a(j):
    return jnp.where(j == 3, 7, j)


def _win_index_b(j):
    return jnp.where(j == 3, 8, jnp.maximum(j, 1) + 2)


def _win_index_c(j):
    return jnp.clip(j, 1, 2) + 4


def _mix_prompt_call(l, x, modp, p):
    tm = TM_PROMPT
    tps = SEQ // tm
    n_tiles = M_PROMPT // tm

    def mod_spec(k):
        return pl.BlockSpec((None, None, None, 1, D_MODEL), lambda i, j: (l, i // tps, k, 0, 0))

    full3 = lambda shape: pl.BlockSpec((None,) + shape, lambda i, j: (l,) + (0,) * len(shape))
    kern = functools.partial(_mix_prompt_kernel, tm=tm, tiles_per_seq=tps)
    return pl.pallas_call(
        kern,
        grid=(n_tiles, 4),
        in_specs=[
            pl.BlockSpec((tm, D_MODEL), lambda i, j: (i, 0)),
            full3((1, D_MODEL)),
            mod_spec(0), mod_spec(1),
            pl.BlockSpec((None, D_MODEL, CB), lambda i, j: (_win_index_a(j), 0, 0)),
            pl.BlockSpec((None, D_MODEL, CB), lambda i, j: (_win_index_b(j), 0, 0)),
            pl.BlockSpec((None, D_MODEL, CB), lambda i, j: (_win_index_c(j), 0, 0)),
            full3((4, GROUP, GROUP)),
            full3((1, POOL_WIDTH)),
            pl.BlockSpec((None, CONV_K, CB), lambda i, j: (l, 0, jnp.clip(j - 1, 0, 1))),
            full3((1, SGU_WIDTH)),
            full3((4, CHUNK, CHUNK)),
            full3((CHUNK, 4)),
        ],
        out_specs=[
            pl.BlockSpec((tm, CB), lambda i, j: (i, j)),
            pl.BlockSpec((None, POOL_HALO, CB), lambda i, j: (i, 0, 0)),
            pl.BlockSpec((None, CONV_HALO, CB), lambda i, j: (i, 0, jnp.clip(j - 1, 0, 1))),
            pl.BlockSpec((CHUNK, SGU_WIDTH), lambda i, j: (i, 0)),
        ],
        out_shape=[
            jax.ShapeDtypeStruct((M_PROMPT, D_MODEL), BF16),
            jax.ShapeDtypeStruct((n_tiles, POOL_HALO, POOL_WIDTH), F32),
            jax.ShapeDtypeStruct((n_tiles, CONV_HALO, CONV_WIDTH), F32),
            jax.ShapeDtypeStruct((n_tiles * CHUNK, SGU_WIDTH), F32),
        ],
        scratch_shapes=[
            pltpu.VMEM((tm, D_MODEL), BF16),
            pltpu.VMEM((tm, CB), F32),
            pltpu.VMEM((tm, CB), F32),
            pltpu.VMEM((tm + POOL_HALO, CB), F32),
            pltpu.VMEM((tm + CONV_HALO, CB), F32),
            pltpu.VMEM((CONV_WIDTH // CB, CONV_HALO, CB), F32),
        ],
        compiler_params=pltpu.CompilerParams(
            dimension_semantics=("arbitrary", "arbitrary"), vmem_limit_bytes=VMEM_LIMIT),
        name="mix_prompt",
    )(x, p["norm1"], modp, modp, p["w_mix"], p["w_mix"], p["w_mix"], p["w_pool_grp"], p["pool_scale"],
      p["w_conv"], p["sgu_norm"], p["w_sgu"], p["b_sgu_t"])


def _mix_sample_kernel(x_ref, n1_ref, sh_ref, sc_ref, wa_ref, wb_ref, wc_ref, wgrp_ref, pscale_ref,
                       wconv_ref, sgun_ref, wv_ref, bv_ref, spool_ref, sconv_ref,
                       br_ref, pool_ref, conv_ref, v_ref,
                       hbuf, t0, t1):
    j = pl.program_id(0)
    nb = DEC_BATCH
    T = DEC_SEQ

    def slab(t):
        return slice(t * nb, (t + 1) * nb)

    @pl.when(j == 0)
    def _():
        _modulated_norm(x_ref, n1_ref, sc_ref, sh_ref, hbuf, T * nb)

    @pl.when(j == 0)
    def _pool():
        t0[...] = _dot(hbuf[...], wa_ref[...])
        keep = POOL_BUF - T
        pool_ref[0:keep * nb, :] = spool_ref[T * nb:POOL_BUF * nb, :]
        pool_ref[keep * nb:POOL_BUF * nb, :] = t0[...]

        def full(s, cols):
            if s < POOL_BUF:
                return spool_ref[slab(s), cols]
            return t0[slab(s - POOL_BUF), cols]

        for t in range(T):
            for g, w in enumerate(POOL_WINDOWS):
                cols = slice(g * GROUP, (g + 1) * GROUP)
                s = full(POOL_BUF + t, cols)
                for k in range(1, w):
                    s = s + full(POOL_BUF + t - k, cols)
                cnt = float(min(PAST_LEN + t + 1, w))
                d = s / cnt - t0[slab(t), cols]
                y = _dot(d.astype(BF16), wgrp_ref[g].astype(BF16)) * pscale_ref[:, cols]
                br_ref[slab(t), cols] = y.astype(BF16)

    @pl.when(jnp.logical_and(j >= 1, j <= 2))
    def _conv():
        h = hbuf[...]
        t1[...] = _dot(h, wc_ref[...]) * _dot(h, wa_ref[...])
        t0[...] = _dot(h, wb_ref[...])

        def full(s):
            if s < CONV_BUF:
                return sconv_ref[slab(s), :]
            return t1[slab(s - CONV_BUF), :]

        for s in range(CONV_BUF):
            conv_ref[slab(s), :] = full(T + s)
        for t in range(T):
            conv = (wconv_ref[0:1, :] * full(t) + wconv_ref[1:2, :] * full(t + 1)
                    + wconv_ref[2:3, :] * full(t + 2))
            br_ref[slab(t), :] = (t0[slab(t), :] * conv).astype(BF16)

    @pl.when(j == 3)
    def _sgu():
        h = hbuf[...]
        t0[...] = _dot(h, wa_ref[...])
        t1[...] = _dot(h, wb_ref[...])
        for t in range(T):
            v_ref[slab(t), :] = _rms_scale(jax.nn.gelu(t1[slab(t), :]), sgun_ref[...])
        for t in range(T):
            sg = bv_ref[t:t + 1, :]
            for k in range(t + 1):
                sg = sg + wv_ref[t * T + k:t * T + k + 1, :] * v_ref[slab(k), :]
            br_ref[slab(t), :] = (jax.nn.gelu(t0[slab(t), :]) * sg).astype(BF16)


def _mix_sample_call(l, x, mod, p, state_pool_tm, state_conv_tm):
    tm = M_SAMPLE

    def mod_spec(k):
        return pl.BlockSpec((None, DEC_BATCH, D_MODEL), lambda j: (l, 0, k))

    full3 = lambda shape: pl.BlockSpec((None,) + shape, lambda j: (l,) + (0,) * len(shape))
    cbi = lambda j: jnp.clip(j - 1, 0, 1)
    return pl.pallas_call(
        _mix_sample_kernel,
        grid=(4,),
        in_specs=[
            pl.BlockSpec((tm, D_MODEL), lambda j: (0, 0)),
            full3((1, D_MODEL)),
            mod_spec(0), mod_spec(1),
            pl.BlockSpec((None, D_MODEL, CB), lambda j: (_win_index_a(j), 0, 0)),
            pl.BlockSpec((None, D_MODEL, CB), lambda j: (_win_index_b(j), 0, 0)),
            pl.BlockSpec((None, D_MODEL, CB), lambda j: (_win_index_c(j), 0, 0)),
            full3((4, GROUP, GROUP)),
            full3((1, POOL_WIDTH)),
            pl.BlockSpec((None, CONV_K, CB), lambda j: (l, 0, cbi(j))),
            full3((1, SGU_WIDTH)),
            full3((DEC_SEQ * DEC_SEQ, SGU_WIDTH)),
            full3((DEC_SEQ, SGU_WIDTH)),
            full3((POOL_BUF * DEC_BATCH, POOL_WIDTH)),
            pl.BlockSpec((None, CONV_BUF * DEC_BATCH, CB), lambda j: (l, 0, cbi(j))),
        ],
        out_specs=[
            pl.BlockSpec((tm, CB), lambda j: (0, j)),
            pl.BlockSpec((POOL_BUF * DEC_BATCH, POOL_WIDTH), lambda j: (0, 0)),
            pl.BlockSpec((CONV_BUF * DEC_BATCH, CB), lambda j: (0, cbi(j))),
            pl.BlockSpec((tm, SGU_WIDTH), lambda j: (0, 0)),
        ],
        out_shape=[
            jax.ShapeDtypeStruct((tm, D_MODEL), BF16),
            jax.ShapeDtypeStruct((POOL_BUF * DEC_BATCH, POOL_WIDTH), F32),
            jax.ShapeDtypeStruct((CONV_BUF * DEC_BATCH, CONV_WIDTH), F32),
            jax.ShapeDtypeStruct((tm, SGU_WIDTH), F32),
        ],
        scratch_shapes=[
            pltpu.VMEM((tm, D_MODEL), BF16),
            pltpu.VMEM((tm, CB), F32),
            pltpu.VMEM((tm, CB), F32),
        ],
        compiler_params=pltpu.CompilerParams(
            dimension_semantics=("arbitrary",), vmem_limit_bytes=VMEM_LIMIT),
        name="mix_sample",
    )(x, p["norm1"], mod, mod, p["w_mix"], p["w_mix"], p["w_mix"], p["w_pool_grp"], p["pool_scale"],
      p["w_conv"], p["sgu_norm"], p["w_sgu_v"], p["b_sgu_v"], state_pool_tm, state_conv_tm)


def _blocked(n_blocks, width, col0=0):
    return [(b, 0, col0 + b * width, width) for b in range(n_blocks)]


_N_GATE_BLOCKS = D_MODEL // TN_GATE
_LAYOUTS = {
    "w_mix": ("w_in", OFF_GATE // CB, CB, _blocked(OFF_GATE // CB, CB)),
    "w_gate": ("w_in", _N_GATE_BLOCKS, 3 * TN_GATE,
               [(j, k * TN_GATE, OFF_GATE + k * D_MODEL + j * TN_GATE, TN_GATE)
                for j in range(_N_GATE_BLOCKS) for k in range(3)]),
    "w_br_pool": ("w_br_pool", _N_GATE_BLOCKS, TN_GATE, _blocked(_N_GATE_BLOCKS, TN_GATE)),
    "w_br_conv": ("w_br_conv", _N_GATE_BLOCKS, TN_GATE, _blocked(_N_GATE_BLOCKS, TN_GATE)),
    "w_br_sgu": ("w_br_sgu", _N_GATE_BLOCKS, TN_GATE, _blocked(_N_GATE_BLOCKS, TN_GATE)),
    "w_out": ("w_out", None, D_MODEL, [(None, 0, 0, D_MODEL)]),
    "w_ff1": ("w_ff1", D_FF // TF, TF, _blocked(D_FF // TF, TF)),
    "w_ff2": ("w_ff2", None, D_MODEL, [(None, 0, 0, D_MODEL)]),
}


def _layout_first_layer(name, big):
    src, n_blocks, _, pieces = _LAYOUTS[name]
    w = big[src][0].astype(BF16)
    if n_blocks is None:
        return w
    blocks = []
    for b in range(n_blocks):
        mine = sorted((pc for pc in pieces if pc[0] == b), key=lambda pc: pc[1])
        blocks.append(jnp.concatenate([w[:, s0:s0 + wd] for (_, _, s0, wd) in mine], axis=1))
    return jnp.stack(blocks)


def _cast_specs(l_next, name, big, n_steps, step_of):
    src, n_blocks, width, pieces = _LAYOUTS[name]
    _, rows, _ = big[src].shape
    src_cols = max(s0 + wd for (_, _, s0, wd) in pieces)
    rows_per = max(BF16_SUBLANES, rows // n_steps)
    reps = n_steps * rows_per // rows
    in_spec = pl.BlockSpec((None, rows_per, src_cols), lambda i, j: (l_next, step_of(i, j) // reps, 0))
    if n_blocks is None:
        out_spec = pl.BlockSpec((rows_per, width), lambda i, j: (step_of(i, j) // reps, 0))
        shape = (rows, width)
    else:
        out_spec = pl.BlockSpec((n_blocks, rows_per, width), lambda i, j: (0, step_of(i, j) // reps, 0))
        shape = (n_blocks, rows, width)
    return in_spec, out_spec, jax.ShapeDtypeStruct(shape, BF16), big[src], tuple(pieces)


def _cast_slices(src_refs, dst_refs, all_pieces):
    for src, dst, pieces in zip(src_refs, dst_refs, all_pieces):
        for b, d0, s0, wd in pieces:
            v = src[:, s0:s0 + wd].astype(BF16)
            if b is None:
                dst[:, d0:d0 + wd] = v
            else:
                dst[b, :, d0:d0 + wd] = v


def _gate_kernel(*refs, tm, n1, cast_pieces):
    n_fixed = 11
    n_cast = len(cast_pieces)
    (x_ref, br_ref, n1_ref, sh_ref, sc_ref, g_ref, wg_ref,
     wbp_ref, wbc_ref, wbs_ref, wout_ref) = refs[:n_fixed]
    cast_src = refs[n_fixed:n_fixed + n_cast]
    o_ref = refs[n_fixed + n_cast]
    cast_dst = refs[n_fixed + 1 + n_cast:n_fixed + 1 + 2 * n_cast]
    hbuf = refs[n_fixed + 1 + 2 * n_cast]
    j = pl.program_id(1)
    half = tm // 2

    @pl.when(j == 0)
    def _():
        _modulated_norm(x_ref, n1_ref, sc_ref, sh_ref, hbuf, tm)
        o_ref[...] = jnp.zeros((tm, D_MODEL), F32)

    _cast_slices(cast_src, cast_dst, cast_pieces)
    for hh in range(2):
        rows = slice(hh * half, (hh + 1) * half)
        gates = jax.nn.sigmoid(_dot(hbuf[rows, :], wg_ref[...]))
        m = gates[:, 0:TN_GATE] * _dot(br_ref[rows, 0:POOL_WIDTH], wbp_ref[...])
        m = m + (gates[:, TN_GATE:2 * TN_GATE]
                 * _dot(br_ref[rows, POOL_WIDTH:POOL_WIDTH + CONV_WIDTH], wbc_ref[...]))
        m = m + (gates[:, 2 * TN_GATE:]
                 * _dot(br_ref[rows, POOL_WIDTH + CONV_WIDTH:], wbs_ref[...]))
        o_ref[rows, :] += _dot(m.astype(BF16), wout_ref[...])

    @pl.when(j == n1 - 1)
    def _():
        def body(c, carry):
            r0 = pl.multiple_of(c * ROWS, ROWS)
            o_ref[pl.ds(r0, ROWS), :] = (x_ref[pl.ds(r0, ROWS), :]
                                         + g_ref[...] * o_ref[pl.ds(r0, ROWS), :])
            return carry
        lax.fori_loop(0, tm // ROWS, body, 0)


def _gate_call(l, x, br, p, grp, cast_from):
    tm, n_tiles, mod_arr, mod_spec = grp["tm"], grp["n_tiles"], grp["mod"], grp["mod_spec"]
    n1 = _N_GATE_BLOCKS

    def branch_spec(width):
        return pl.BlockSpec((None, width, TN_GATE), lambda i, j: (j, 0, 0))

    casts = []
    if cast_from is not None:
        casts = [_cast_specs(l + 1, name, cast_from, n_tiles * n1, lambda i, j: i * n1 + j)
                 for name in _GATE_CAST]
    kern = functools.partial(_gate_kernel, tm=tm, n1=n1, cast_pieces=tuple(c[4] for c in casts))
    outs = pl.pallas_call(
        kern,
        grid=(n_tiles, n1),
        in_specs=[
            pl.BlockSpec((tm, D_MODEL), lambda i, j: (i, 0), pipeline_mode=pl.Buffered(1)),
            pl.BlockSpec((tm, D_MODEL), lambda i, j: (i, 0), pipeline_mode=pl.Buffered(1)),
            pl.BlockSpec((None, 1, D_MODEL), lambda i, j: (l, 0, 0)),
            mod_spec(l, 0), mod_spec(l, 1), mod_spec(l, 2),
            pl.BlockSpec((None, D_MODEL, 3 * TN_GATE), lambda i, j: (j, 0, 0)),
            branch_spec(POOL_WIDTH), branch_spec(CONV_WIDTH), branch_spec(SGU_WIDTH),
            pl.BlockSpec((TN_GATE, D_MODEL), lambda i, j: (j, 0)),
        ] + [c[0] for c in casts],
        out_specs=[pl.BlockSpec((tm, D_MODEL), lambda i, j: (i, 0))] + [c[1] for c in casts],
        out_shape=[jax.ShapeDtypeStruct(x.shape, F32)] + [c[2] for c in casts],
        scratch_shapes=[pltpu.VMEM((tm, D_MODEL), BF16)],
        compiler_params=pltpu.CompilerParams(
            dimension_semantics=("arbitrary", "arbitrary"), vmem_limit_bytes=VMEM_LIMIT),
        name="gate_" + grp["name"],
    )(x, br, p["norm1"], mod_arr, mod_arr, mod_arr, p["w_gate"],
      p["w_br_pool"], p["w_br_conv"], p["w_br_sgu"], p["w_out"], *[c[3] for c in casts])
    return outs[0], outs[1:]


def _ffn_kernel(*refs, tm, nf, final_norm, cast_pieces):
    n_cast = len(cast_pieces)
    x_ref, n2_ref, sh_ref, sc_ref, g_ref, fn_ref, w1_ref, w2_ref = refs[:8]
    cast_src = refs[8:8 + n_cast]
    o_ref = refs[8 + n_cast]
    cast_dst = refs[9 + n_cast:9 + 2 * n_cast]
    hbuf = refs[9 + 2 * n_cast]
    f = pl.program_id(1)
    half = tm // 2

    @pl.when(f == 0)
    def _():
        _modulated_norm(x_ref, n2_ref, sc_ref, sh_ref, hbuf, tm)
        o_ref[...] = jnp.zeros((tm, D_MODEL), F32)

    _cast_slices(cast_src, cast_dst, cast_pieces)

    for hh in range(2):
        rows = slice(hh * half, (hh + 1) * half)
        a = jnp.square(jax.nn.relu(_dot(hbuf[rows, :], w1_ref[...]))).astype(BF16)
        o_ref[rows, :] += _dot(a, w2_ref[...])

    @pl.when(f == nf - 1)
    def _():
        def body(c, carry):
            r0 = pl.multiple_of(c * ROWS, ROWS)
            y = x_ref[pl.ds(r0, ROWS), :] + g_ref[...] * o_ref[pl.ds(r0, ROWS), :]
            if final_norm:
                y = _rms_scale(y, fn_ref[...])
            o_ref[pl.ds(r0, ROWS), :] = y
            return carry
        lax.fori_loop(0, tm // ROWS, body, 0)


def _ffn_call(l, x, p, grp, final_norm, cast_from):
    tm, n_tiles, mod_arr, mod_spec = grp["tm"], grp["n_tiles"], grp["mod"], grp["mod_spec"]
    nf = D_FF // TF
    casts = []
    if cast_from is not None:
        casts = [_cast_specs(l + 1, name, cast_from, n_tiles * nf, lambda i, f: i * nf + f)
                 for name in _FFN_CAST]
    kern = functools.partial(_ffn_kernel, tm=tm, nf=nf, final_norm=final_norm,
                             cast_pieces=tuple(c[4] for c in casts))
    outs = pl.pallas_call(
        kern,
        grid=(n_tiles, nf),
        in_specs=[
            pl.BlockSpec((tm, D_MODEL), lambda i, f: (i, 0)),
            pl.BlockSpec((None, 1, D_MODEL), lambda i, f: (l, 0, 0)),
            mod_spec(l, 3), mod_spec(l, 4), mod_spec(l, 5),
            pl.BlockSpec((1, D_MODEL), lambda i, f: (0, 0)),
            pl.BlockSpec((None, D_MODEL, TF), lambda i, f: (f, 0, 0)),
            pl.BlockSpec((TF, D_MODEL), lambda i, f: (f, 0)),
        ] + [c[0] for c in casts],
        out_specs=[pl.BlockSpec((tm, D_MODEL), lambda i, f: (i, 0))] + [c[1] for c in casts],
        out_shape=[jax.ShapeDtypeStruct(x.shape, F32)] + [c[2] for c in casts],
        scratch_shapes=[pltpu.VMEM((tm, D_MODEL), BF16)],
        compiler_params=pltpu.CompilerParams(
            dimension_semantics=("arbitrary", "arbitrary"), vmem_limit_bytes=VMEM_LIMIT),
        name="ffn_" + grp["name"],
    )(x, p["norm2"], mod_arr, mod_arr, mod_arr, p["final_norm"], p["w_ff1"], p["w_ff2"],
      *[c[3] for c in casts])
    return outs[0], outs[1:]


_GATE_CAST = ("w_mix", "w_gate", "w_br_pool", "w_br_conv", "w_br_sgu", "w_out", "w_ff2")
_FFN_CAST = ("w_ff1",)


def kernel(x_prompt, x_sample, state_pool, state_conv, c_prompt, c_sample, norm1, norm2, w_ada, b_ada,
           w_in, w_pool_grp, pool_scale, w_conv, sgu_norm, w_sgu, b_sgu, w_br_pool, w_br_conv,
           w_br_sgu, w_out, w_ff1, w_ff2, final_norm):
    T, nb = DEC_SEQ, DEC_BATCH
    tps = SEQ // TM_PROMPT

    c_all = jnp.concatenate(
        [c_sample, c_prompt, jnp.zeros((ADA_ROWS - nb - BATCH, D_MODEL), F32)], axis=0)
    mod = _ada_call(c_all, w_ada, b_ada)
    modp = mod[:, nb:nb + BATCH].reshape(DEPTH, BATCH, N_MOD, 1, D_MODEL)

    small = {
        "norm1": norm1.reshape(DEPTH, 1, D_MODEL),
        "norm2": norm2.reshape(DEPTH, 1, D_MODEL),
        "final_norm": final_norm.reshape(1, D_MODEL),
        "w_pool_grp": w_pool_grp,
        "pool_scale": pool_scale.reshape(DEPTH, 1, POOL_WIDTH),
        "w_conv": w_conv,
        "sgu_norm": sgu_norm.reshape(DEPTH, 1, SGU_WIDTH),
        "w_sgu": w_sgu,
        "b_sgu_t": jnp.transpose(b_sgu, (0, 2, 1)),
        "w_sgu_v": jnp.repeat(
            jnp.transpose(w_sgu[:, :, :T, :T], (0, 2, 3, 1)).reshape(DEPTH, T * T, 4), GROUP, axis=-1),
        "b_sgu_v": jnp.repeat(jnp.transpose(b_sgu[:, :, :T], (0, 2, 1)), GROUP, axis=-1),
    }
    big = {"w_in": w_in, "w_br_pool": w_br_pool, "w_br_conv": w_br_conv, "w_br_sgu": w_br_sgu,
           "w_out": w_out, "w_ff1": w_ff1, "w_ff2": w_ff2}
    wl = {name: _layout_first_layer(name, big) for name in _GATE_CAST + _FFN_CAST}

    grp_p = {
        "name": "prompt", "tm": TM_PROMPT, "n_tiles": M_PROMPT // TM_PROMPT, "mod": modp,
        "mod_spec": lambda l, k: pl.BlockSpec(
            (None, None, None, 1, D_MODEL), lambda i, j: (l, i // tps, k, 0, 0)),
    }
    grp_s = {
        "name": "sample", "tm": M_SAMPLE, "n_tiles": 1, "mod": mod,
        "mod_spec": lambda l, k: pl.BlockSpec((None, nb, D_MODEL), lambda i, j: (l, 0, k)),
    }

    xp = x_prompt.reshape(M_PROMPT, D_MODEL)
    xs = jnp.transpose(x_sample, (1, 0, 2)).reshape(M_SAMPLE, D_MODEL)
    spool_tm = jnp.transpose(state_pool, (0, 2, 1, 3)).reshape(DEPTH, POOL_BUF * nb, POOL_WIDTH)
    sconv_tm = jnp.transpose(state_conv, (0, 2, 1, 3)).reshape(DEPTH, CONV_BUF * nb, CONV_WIDTH)

    pool_p, conv_p, v_p, pool_s, conv_s, v_s = [], [], [], [], [], []
    last_tile = slice(tps - 1, None, tps)
    for l in range(DEPTH):
        final = l == DEPTH - 1
        p = dict(small, **wl)
        gate_cast = None if final else big
        ffn_cast = None if final else big

        br, ptail, ztail, vtail = _mix_prompt_call(l, xp, modp, p)
        xp, gate_next = _gate_call(l, xp, br, p, grp_p, gate_cast)
        xp, ffn_next = _ffn_call(l, xp, p, grp_p, final, ffn_cast)
        pool_p.append(ptail[last_tile, POOL_HALO - POOL_BUF:, :])
        conv_p.append(ztail[last_tile, CONV_HALO - CONV_BUF:, :])
        v_p.append(vtail.reshape(-1, CHUNK, SGU_WIDTH)[last_tile])

        br, npool, nconv, nv = _mix_sample_call(l, xs, mod, p, spool_tm, sconv_tm)
        xs, _ = _gate_call(l, xs, br, p, grp_s, None)
        xs, _ = _ffn_call(l, xs, p, grp_s, final, None)
        pool_s.append(jnp.transpose(npool.reshape(POOL_BUF, nb, POOL_WIDTH), (1, 0, 2)))
        conv_s.append(jnp.transpose(nconv.reshape(CONV_BUF, nb, CONV_WIDTH), (1, 0, 2)))
        v_s.append(jnp.transpose(nv.reshape(T, nb, SGU_WIDTH), (1, 0, 2)))
        if not final:
            wl = dict(zip(_GATE_CAST + _FFN_CAST, list(gate_next) + list(ffn_next)))

    y_prompt = xp.reshape(BATCH, SEQ, D_MODEL)
    y_sample = jnp.transpose(xs.reshape(T, nb, D_MODEL), (1, 0, 2))
    return (y_prompt, y_sample, jnp.stack(pool_p), jnp.stack(conv_p), jnp.stack(v_p),
            jnp.stack(pool_s), jnp.stack(conv_s), jnp.stack(v_s))
```

```python
import functools

import jax
import jax.numpy as jnp
from jax import lax
from jax.experimental import pallas as pl
from jax.experimental.pallas import tpu as pltpu

F32 = jnp.float32
BF16 = jnp.bfloat16

D_MODEL = 2048
DEPTH = 4
BATCH = 4
SEQ = 2048
DEC_BATCH = 128
DEC_SEQ = 4
PAST_LEN = 16384
POOL_WINDOWS = (2, 4, 8, 16)
GROUP = 128
POOL_WIDTH = 512
POOL_BUF = 15
CONV_WIDTH = 1024
CONV_K = 3
CONV_BUF = 2
CHUNK = 128
SGU_WIDTH = 512
N_MOD = 6
D_FF = 4 * D_MODEL
EPS = 1e-6
OFF_GATE = POOL_WIDTH + 3 * CONV_WIDTH + 2 * SGU_WIDTH
N_IN = OFF_GATE + 3 * D_MODEL

ROWS = 128
CB = 512
POOL_HALO = 16
CONV_HALO = 8
TM_PROMPT = 1024
TM_MIX = 512
M_PROMPT = BATCH * SEQ
M_SAMPLE = DEC_BATCH * DEC_SEQ
TN_GATE = 256
TF = 1024
BF16_SUBLANES = 16
TN_ADA = 1024
ADA_ROWS = 136
VMEM_LIMIT = 60 * 1024 * 1024


def _dot(a, b):
    return jnp.dot(a, b, preferred_element_type=F32)


def _rms_scale(x, g):
    return x * lax.rsqrt(jnp.mean(x * x, axis=-1, keepdims=True) + EPS) * g


def _modulated_norm(x_ref, g_ref, sc_ref, sh_ref, h_ref, tm):
    def body(c, carry):
        r0 = pl.multiple_of(c * ROWS, ROWS)
        x = x_ref[pl.ds(r0, ROWS), :]
        r = lax.rsqrt(jnp.mean(x * x, axis=-1, keepdims=True) + EPS)
        h = x_ref[pl.ds(r0, ROWS), :] * r * (g_ref[...] * (1.0 + sc_ref[...])) + sh_ref[...]
        h_ref[pl.ds(r0, ROWS), :] = h.astype(BF16)
        return carry
    lax.fori_loop(0, tm // ROWS, body, 0)


def _ada_kernel(c_ref, w_ref, b_ref, o_ref):
    a = jax.nn.silu(c_ref[...]).astype(BF16)
    o_ref[...] = _dot(a, w_ref[...].astype(BF16)) + b_ref[...]


def _ada_call(c_all, w_ada, b_ada):
    n = N_MOD * D_MODEL
    return pl.pallas_call(
        _ada_kernel,
        grid=(DEPTH, n // TN_ADA),
        in_specs=[
            pl.BlockSpec((ADA_ROWS, D_MODEL), lambda l, j: (0, 0)),
            pl.BlockSpec((None, D_MODEL, TN_ADA), lambda l, j: (l, 0, j)),
            pl.BlockSpec((None, 1, TN_ADA), lambda l, j: (l, 0, j)),
        ],
        out_specs=pl.BlockSpec((None, ADA_ROWS, TN_ADA), lambda l, j: (l, 0, j)),
        out_shape=jax.ShapeDtypeStruct((DEPTH, ADA_ROWS, n), F32),
        compiler_params=pltpu.CompilerParams(
            dimension_semantics=("arbitrary", "arbitrary"), vmem_limit_bytes=VMEM_LIMIT),
        name="ada",
    )(c_all, w_ada, b_ada.reshape(DEPTH, 1, n))


def _mix_prompt_kernel(x_ref, n1_ref, sh_ref, sc_ref, w_ref, wgrp_ref, pscale_ref,
                       wconv_ref, sgun_ref, wsgu_ref, bsgu_ref,
                       br_ref, ptail_ref, ztail_ref, vtail_ref,
                       hbuf, ubuf, vbuf, pbuf, zbuf, bcbuf, *, tm, tiles_per_seq):
    tile_in_seq = pl.program_id(0) % tiles_per_seq
    first_tile = tile_in_seq == 0
    n_slabs = tm // ROWS
    n_cb = CONV_WIDTH // CB
    slab = lambda c: slice(c * ROWS, (c + 1) * ROWS)
    proj = lambda b: _dot(hbuf[...], w_ref[b])

    @pl.when(pl.program_id(0) == 0)
    def _():
        pbuf[tm:tm + POOL_HALO, :] = jnp.zeros((POOL_HALO, CB), F32)
        zbuf[:, tm:tm + CONV_HALO, :] = jnp.zeros((n_cb, CONV_HALO, CB), F32)

    _modulated_norm(x_ref, n1_ref, sc_ref, sh_ref, hbuf, tm)

    pbuf[0:POOL_HALO, :] = jnp.where(first_tile, 0.0, pbuf[tm:tm + POOL_HALO, :])
    for cb in range(n_cb):
        zbuf[cb, 0:CONV_HALO, :] = jnp.where(first_tile, 0.0, zbuf[cb, tm:tm + CONV_HALO, :])

    ubuf[...] = proj(7)
    vbuf[...] = proj(8)
    pbuf[POOL_HALO:POOL_HALO + tm, :] = proj(0)
    ptail_ref[...] = pbuf[tm:tm + POOL_HALO, :]

    row = lax.broadcasted_iota(jnp.int32, (CHUNK, CHUNK), 0)
    col = lax.broadcasted_iota(jnp.int32, (CHUNK, CHUNK), 1)
    n_grp = SGU_WIDTH // GROUP
    wt = [jnp.where(row >= col, wsgu_ref[g], 0.0).astype(BF16) for g in range(n_grp)]
    for c in range(n_slabs):
        vn = _rms_scale(jax.nn.gelu(vbuf[slab(c), :]), sgun_ref[...])
        if c == n_slabs - 1:
            vtail_ref[...] = vn
        for g in range(n_grp):
            cols = slice(g * GROUP, (g + 1) * GROUP)
            sg = _dot(wt[g], vn[:, cols].astype(BF16)) + bsgu_ref[:, g:g + 1]
            out = jax.nn.gelu(ubuf[slab(c), cols]) * sg
            br_ref[slab(c), POOL_WIDTH + CONV_WIDTH + g * GROUP:
                   POOL_WIDTH + CONV_WIDTH + (g + 1) * GROUP] = out.astype(BF16)

    def conv_proj(cb):
        zbuf[cb, CONV_HALO:CONV_HALO + tm, :] = proj(5 + cb) * proj(1 + cb)
        bcbuf[cb] = proj(3 + cb)
        ztail_ref[:, cb * CB:(cb + 1) * CB] = zbuf[cb, tm:tm + CONV_HALO, :]

    def conv_mixer(cb):
        wcols = slice(cb * CB, (cb + 1) * CB)
        for c in range(n_slabs):
            ext = zbuf[cb, c * ROWS:(c + 1) * ROWS + CONV_HALO, :]
            conv = (wconv_ref[0:1, wcols] * pltpu.roll(ext, 2, axis=0)[CONV_HALO:, :]
                    + wconv_ref[1:2, wcols] * pltpu.roll(ext, 1, axis=0)[CONV_HALO:, :]
                    + wconv_ref[2:3, wcols] * ext[CONV_HALO:, :])
            br_ref[slab(c), POOL_WIDTH + cb * CB:POOL_WIDTH + (cb + 1) * CB] = (
                bcbuf[cb, slab(c), :] * conv).astype(BF16)

    conv_proj(0)
    pos0 = tile_in_seq * tm
    for c in range(n_slabs):
        pos = pos0 + c * ROWS + lax.broadcasted_iota(jnp.int32, (ROWS, 1), 0)
        for g, w in enumerate(POOL_WINDOWS):
            cols = slice(g * GROUP, (g + 1) * GROUP)
            ext = pbuf[c * ROWS:(c + 1) * ROWS + POOL_HALO, cols]
            s = ext
            k = 1
            while k < w:
                s = s + pltpu.roll(s, k, axis=0)
                k *= 2
            cnt = jnp.minimum(pos + 1, w).astype(F32)
            d = s[POOL_HALO:, :] / cnt - ext[POOL_HALO:, :]
            y = _dot(d.astype(BF16), wgrp_ref[g].astype(BF16)) * pscale_ref[:, cols]
            br_ref[slab(c), cols] = y.astype(BF16)

    conv_proj(1)
    conv_mixer(0)
    conv_mixer(1)


def _win_index_a(j):
    return jnp.where(j == 3, 7, j)


def _win_index_b(j):
    return jnp.where(j == 3, 8, jnp.maximum(j, 1) + 2)


def _win_index_c(j):
    return jnp.clip(j, 1, 2) + 4


def _mix_prompt_call(l, x, modp, p):
    tm = TM_MIX
    tps = SEQ // tm
    n_tiles = M_PROMPT // tm
    n_cb = CONV_WIDTH // CB

    def mod_spec(k):
        return pl.BlockSpec((None, None, None, 1, D_MODEL), lambda i: (l, i // tps, k, 0, 0))

    full3 = lambda shape: pl.BlockSpec((None,) + shape, lambda i: (l,) + (0,) * len(shape))
    kern = functools.partial(_mix_prompt_kernel, tm=tm, tiles_per_seq=tps)
    return pl.pallas_call(
        kern,
        grid=(n_tiles,),
        in_specs=[
            pl.BlockSpec((tm, D_MODEL), lambda i: (i, 0)),
            full3((1, D_MODEL)),
            mod_spec(0), mod_spec(1),
            pl.BlockSpec((OFF_GATE // CB, D_MODEL, CB), lambda i: (0, 0, 0)),
            full3((4, GROUP, GROUP)),
            full3((1, POOL_WIDTH)),
            full3((CONV_K, CONV_WIDTH)),
            full3((1, SGU_WIDTH)),
            full3((4, CHUNK, CHUNK)),
            full3((CHUNK, 4)),
        ],
        out_specs=[
            pl.BlockSpec((tm, D_MODEL), lambda i: (i, 0)),
            pl.BlockSpec((None, POOL_HALO, POOL_WIDTH), lambda i: (i, 0, 0)),
            pl.BlockSpec((None, CONV_HALO, CONV_WIDTH), lambda i: (i, 0, 0)),
            pl.BlockSpec((CHUNK, SGU_WIDTH), lambda i: (i, 0)),
        ],
        out_shape=[
            jax.ShapeDtypeStruct((M_PROMPT, D_MODEL), BF16),
            jax.ShapeDtypeStruct((n_tiles, POOL_HALO, POOL_WIDTH), F32),
            jax.ShapeDtypeStruct((n_tiles, CONV_HALO, CONV_WIDTH), F32),
            jax.ShapeDtypeStruct((n_tiles * CHUNK, SGU_WIDTH), F32),
        ],
        scratch_shapes=[
            pltpu.VMEM((tm, D_MODEL), BF16),
            pltpu.VMEM((tm, SGU_WIDTH), F32),
            pltpu.VMEM((tm, SGU_WIDTH), F32),
            pltpu.VMEM((tm + POOL_HALO, POOL_WIDTH), F32),
            pltpu.VMEM((n_cb, tm + CONV_HALO, CB), F32),
            pltpu.VMEM((n_cb, tm, CB), F32),
        ],
        compiler_params=pltpu.CompilerParams(
            dimension_semantics=("arbitrary",), vmem_limit_bytes=VMEM_LIMIT),
        name="mix_prompt",
    )(x, p["norm1"], modp, modp, p["w_mix"], p["w_pool_grp"], p["pool_scale"],
      p["w_conv"], p["sgu_norm"], p["w_sgu"], p["b_sgu_t"])


def _mix_sample_kernel(x_ref, n1_ref, sh_ref, sc_ref, wa_ref, wb_ref, wc_ref, wgrp_ref, pscale_ref,
                       wconv_ref, sgun_ref, wv_ref, bv_ref, spool_ref, sconv_ref,
                       br_ref, pool_ref, conv_ref, v_ref,
                       hbuf, t0, t1):
    j = pl.program_id(0)
    nb = DEC_BATCH
    T = DEC_SEQ

    def slab(t):
        return slice(t * nb, (t + 1) * nb)

    @pl.when(j == 0)
    def _():
        _modulated_norm(x_ref, n1_ref, sc_ref, sh_ref, hbuf, T * nb)

    @pl.when(j == 0)
    def _pool():
        t0[...] = _dot(hbuf[...], wa_ref[...])
        keep = POOL_BUF - T
        pool_ref[0:keep * nb, :] = spool_ref[T * nb:POOL_BUF * nb, :]
        pool_ref[keep * nb:POOL_BUF * nb, :] = t0[...]

        def full(s, cols):
            if s < POOL_BUF:
                return spool_ref[slab(s), cols]
            return t0[slab(s - POOL_BUF), cols]

        for t in range(T):
            for g, w in enumerate(POOL_WINDOWS):
                cols = slice(g * GROUP, (g + 1) * GROUP)
                s = full(POOL_BUF + t, cols)
                for k in range(1, w):
                    s = s + full(POOL_BUF + t - k, cols)
                cnt = float(min(PAST_LEN + t + 1, w))
                d = s / cnt - t0[slab(t), cols]
                y = _dot(d.astype(BF16), wgrp_ref[g].astype(BF16)) * pscale_ref[:, cols]
                br_ref[slab(t), cols] = y.astype(BF16)

    @pl.when(jnp.logical_and(j >= 1, j <= 2))
    def _conv():
        h = hbuf[...]
        t1[...] = _dot(h, wc_ref[...]) * _dot(h, wa_ref[...])
        t0[...] = _dot(h, wb_ref[...])

        def full(s):
            if s < CONV_BUF:
                return sconv_ref[slab(s), :]
            return t1[slab(s - CONV_BUF), :]

        for s in range(CONV_BUF):
            conv_ref[slab(s), :] = full(T + s)
        for t in range(T):
            conv = (wconv_ref[0:1, :] * full(t) + wconv_ref[1:2, :] * full(t + 1)
                    + wconv_ref[2:3, :] * full(t + 2))
            br_ref[slab(t), :] = (t0[slab(t), :] * conv).astype(BF16)

    @pl.when(j == 3)
    def _sgu():
        h = hbuf[...]
        t0[...] = _dot(h, wa_ref[...])
        t1[...] = _dot(h, wb_ref[...])
        for t in range(T):
            v_ref[slab(t), :] = _rms_scale(jax.nn.gelu(t1[slab(t), :]), sgun_ref[...])
        for t in range(T):
            sg = bv_ref[t:t + 1, :]
            for k in range(t + 1):
                sg = sg + wv_ref[t * T + k:t * T + k + 1, :] * v_ref[slab(k), :]
            br_ref[slab(t), :] = (jax.nn.gelu(t0[slab(t), :]) * sg).astype(BF16)


def _mix_sample_call(l, x, mod, p, state_pool_tm, state_conv_tm):
    tm = M_SAMPLE

    def mod_spec(k):
        return pl.BlockSpec((None, DEC_BATCH, D_MODEL), lambda j: (l, 0, k))

    full3 = lambda shape: pl.BlockSpec((None,) + shape, lambda j: (l,) + (0,) * len(shape))
    cbi = lambda j: jnp.clip(j - 1, 0, 1)
    return pl.pallas_call(
        _mix_sample_kernel,
        grid=(4,),
        in_specs=[
            pl.BlockSpec((tm, D_MODEL), lambda j: (0, 0)),
            full3((1, D_MODEL)),
            mod_spec(0), mod_spec(1),
            pl.BlockSpec((None, D_MODEL, CB), lambda j: (_win_index_a(j), 0, 0)),
            pl.BlockSpec((None, D_MODEL, CB), lambda j: (_win_index_b(j), 0, 0)),
            pl.BlockSpec((None, D_MODEL, CB), lambda j: (_win_index_c(j), 0, 0)),
            full3((4, GROUP, GROUP)),
            full3((1, POOL_WIDTH)),
            pl.BlockSpec((None, CONV_K, CB), lambda j: (l, 0, cbi(j))),
            full3((1, SGU_WIDTH)),
            full3((DEC_SEQ * DEC_SEQ, SGU_WIDTH)),
            full3((DEC_SEQ, SGU_WIDTH)),
            full3((POOL_BUF * DEC_BATCH, POOL_WIDTH)),
            pl.BlockSpec((None, CONV_BUF * DEC_BATCH, CB), lambda j: (l, 0, cbi(j))),
        ],
        out_specs=[
            pl.BlockSpec((tm, CB), lambda j: (0, j)),
            pl.BlockSpec((POOL_BUF * DEC_BATCH, POOL_WIDTH), lambda j: (0, 0)),
            pl.BlockSpec((CONV_BUF * DEC_BATCH, CB), lambda j: (0, cbi(j))),
            pl.BlockSpec((tm, SGU_WIDTH), lambda j: (0, 0)),
        ],
        out_shape=[
            jax.ShapeDtypeStruct((tm, D_MODEL), BF16),
            jax.ShapeDtypeStruct((POOL_BUF * DEC_BATCH, POOL_WIDTH), F32),
            jax.ShapeDtypeStruct((CONV_BUF * DEC_BATCH, CONV_WIDTH), F32),
            jax.ShapeDtypeStruct((tm, SGU_WIDTH), F32),
        ],
        scratch_shapes=[
            pltpu.VMEM((tm, D_MODEL), BF16),
            pltpu.VMEM((tm, CB), F32),
            pltpu.VMEM((tm, CB), F32),
        ],
        compiler_params=pltpu.CompilerParams(
            dimension_semantics=("arbitrary",), vmem_limit_bytes=VMEM_LIMIT),
        name="mix_sample",
    )(x, p["norm1"], mod, mod, p["w_mix"], p["w_mix"], p["w_mix"], p["w_pool_grp"], p["pool_scale"],
      p["w_conv"], p["sgu_norm"], p["w_sgu_v"], p["b_sgu_v"], state_pool_tm, state_conv_tm)


def _blocked(n_blocks, width, col0=0):
    return [(b, 0, col0 + b * width, width) for b in range(n_blocks)]


_N_GATE_BLOCKS = D_MODEL // TN_GATE
_LAYOUTS = {
    "w_mix": ("w_in", OFF_GATE // CB, CB, _blocked(OFF_GATE // CB, CB)),
    "w_gate": ("w_in", _N_GATE_BLOCKS, 3 * TN_GATE,
               [(j, k * TN_GATE, OFF_GATE + k * D_MODEL + j * TN_GATE, TN_GATE)
                for j in range(_N_GATE_BLOCKS) for k in range(3)]),
    "w_br_pool": ("w_br_pool", _N_GATE_BLOCKS, TN_GATE, _blocked(_N_GATE_BLOCKS, TN_GATE)),
    "w_br_conv": ("w_br_conv", _N_GATE_BLOCKS, TN_GATE, _blocked(_N_GATE_BLOCKS, TN_GATE)),
    "w_br_sgu": ("w_br_sgu", _N_GATE_BLOCKS, TN_GATE, _blocked(_N_GATE_BLOCKS, TN_GATE)),
    "w_out": ("w_out", None, D_MODEL, [(None, 0, 0, D_MODEL)]),
    "w_ff1": ("w_ff1", D_FF // TF, TF, _blocked(D_FF // TF, TF)),
    "w_ff2": ("w_ff2", None, D_MODEL, [(None, 0, 0, D_MODEL)]),
}


def _layout_first_layer(name, big):
    src, n_blocks, _, pieces = _LAYOUTS[name]
    w = big[src][0].astype(BF16)
    if n_blocks is None:
        return w
    blocks = []
    for b in range(n_blocks):
        mine = sorted((pc for pc in pieces if pc[0] == b), key=lambda pc: pc[1])
        blocks.append(jnp.concatenate([w[:, s0:s0 + wd] for (_, _, s0, wd) in mine], axis=1))
    return jnp.stack(blocks)


def _cast_specs(l_next, name, big, n_steps, step_of):
    src, n_blocks, width, pieces = _LAYOUTS[name]
    _, rows, _ = big[src].shape
    src_cols = max(s0 + wd for (_, _, s0, wd) in pieces)
    rows_per = max(BF16_SUBLANES, rows // n_steps)
    reps = n_steps * rows_per // rows
    in_spec = pl.BlockSpec((None, rows_per, src_cols), lambda i, j: (l_next, step_of(i, j) // reps, 0))
    if n_blocks is None:
        out_spec = pl.BlockSpec((rows_per, width), lambda i, j: (step_of(i, j) // reps, 0))
        shape = (rows, width)
    else:
        out_spec = pl.BlockSpec((n_blocks, rows_per, width), lambda i, j: (0, step_of(i, j) // reps, 0))
        shape = (n_blocks, rows, width)
    return in_spec, out_spec, jax.ShapeDtypeStruct(shape, BF16), big[src], tuple(pieces)


def _cast_slices(src_refs, dst_refs, all_pieces):
    for src, dst, pieces in zip(src_refs, dst_refs, all_pieces):
        for b, d0, s0, wd in pieces:
            v = src[:, s0:s0 + wd].astype(BF16)
            if b is None:
                dst[:, d0:d0 + wd] = v
            else:
                dst[b, :, d0:d0 + wd] = v


def _gate_kernel(*refs, tm, n1, cast_pieces):
    n_fixed = 11
    n_cast = len(cast_pieces)
    (x_ref, br_ref, n1_ref, sh_ref, sc_ref, g_ref, wg_ref,
     wbp_ref, wbc_ref, wbs_ref, wout_ref) = refs[:n_fixed]
    cast_src = refs[n_fixed:n_fixed + n_cast]
    o_ref = refs[n_fixed + n_cast]
    cast_dst = refs[n_fixed + 1 + n_cast:n_fixed + 1 + 2 * n_cast]
    hbuf = refs[n_fixed + 1 + 2 * n_cast]
    j = pl.program_id(1)
    half = tm // 2

    @pl.when(j == 0)
    def _():
        _modulated_norm(x_ref, n1_ref, sc_ref, sh_ref, hbuf, tm)
        o_ref[...] = jnp.zeros((tm, D_MODEL), F32)

    _cast_slices(cast_src, cast_dst, cast_pieces)
    for hh in range(2):
        rows = slice(hh * half, (hh + 1) * half)
        gates = jax.nn.sigmoid(_dot(hbuf[rows, :], wg_ref[...]))
        m = gates[:, 0:TN_GATE] * _dot(br_ref[rows, 0:POOL_WIDTH], wbp_ref[...])
        m = m + (gates[:, TN_GATE:2 * TN_GATE]
                 * _dot(br_ref[rows, POOL_WIDTH:POOL_WIDTH + CONV_WIDTH], wbc_ref[...]))
        m = m + (gates[:, 2 * TN_GATE:]
                 * _dot(br_ref[rows, POOL_WIDTH + CONV_WIDTH:], wbs_ref[...]))
        o_ref[rows, :] += _dot(m.astype(BF16), wout_ref[...])

    @pl.when(j == n1 - 1)
    def _():
        def body(c, carry):
            r0 = pl.multiple_of(c * ROWS, ROWS)
            o_ref[pl.ds(r0, ROWS), :] = (x_ref[pl.ds(r0, ROWS), :]
                                         + g_ref[...] * o_ref[pl.ds(r0, ROWS), :])
            return carry
        lax.fori_loop(0, tm // ROWS, body, 0)


def _gate_call(l, x, br, p, grp, cast_from):
    tm, n_tiles, mod_arr, mod_spec = grp["tm"], grp["n_tiles"], grp["mod"], grp["mod_spec"]
    n1 = _N_GATE_BLOCKS

    def branch_spec(width):
        return pl.BlockSpec((None, width, TN_GATE), lambda i, j: (j, 0, 0))

    casts = []
    if cast_from is not None:
        casts = [_cast_specs(l + 1, name, cast_from, n_tiles * n1, lambda i, j: i * n1 + j)
                 for name in _GATE_CAST]
    kern = functools.partial(_gate_kernel, tm=tm, n1=n1, cast_pieces=tuple(c[4] for c in casts))
    outs = pl.pallas_call(
        kern,
        grid=(n_tiles, n1),
        in_specs=[
            pl.BlockSpec((tm, D_MODEL), lambda i, j: (i, 0), pipeline_mode=pl.Buffered(1)),
            pl.BlockSpec((tm, D_MODEL), lambda i, j: (i, 0)),
            pl.BlockSpec((None, 1, D_MODEL), lambda i, j: (l, 0, 0)),
            mod_spec(l, 0), mod_spec(l, 1), mod_spec(l, 2),
            pl.BlockSpec((None, D_MODEL, 3 * TN_GATE), lambda i, j: (j, 0, 0)),
            branch_spec(POOL_WIDTH), branch_spec(CONV_WIDTH), branch_spec(SGU_WIDTH),
            pl.BlockSpec((TN_GATE, D_MODEL), lambda i, j: (j, 0)),
        ] + [c[0] for c in casts],
        out_specs=[pl.BlockSpec((tm, D_MODEL), lambda i, j: (i, 0))] + [c[1] for c in casts],
        out_shape=[jax.ShapeDtypeStruct(x.shape, F32)] + [c[2] for c in casts],
        scratch_shapes=[pltpu.VMEM((tm, D_MODEL), BF16)],
        compiler_params=pltpu.CompilerParams(
            dimension_semantics=("arbitrary", "arbitrary"), vmem_limit_bytes=VMEM_LIMIT),
        name="gate_" + grp["name"],
    )(x, br, p["norm1"], mod_arr, mod_arr, mod_arr, p["w_gate"],
      p["w_br_pool"], p["w_br_conv"], p["w_br_sgu"], p["w_out"], *[c[3] for c in casts])
    return outs[0], outs[1:]


def _ffn_kernel(*refs, tm, nf, final_norm, cast_pieces):
    n_cast = len(cast_pieces)
    x_ref, n2_ref, sh_ref, sc_ref, g_ref, fn_ref, w1_ref, w2_ref = refs[:8]
    cast_src = refs[8:8 + n_cast]
    o_ref = refs[8 + n_cast]
    cast_dst = refs[9 + n_cast:9 + 2 * n_cast]
    hbuf = refs[9 + 2 * n_cast]
    f = pl.program_id(1)
    half = tm // 2

    @pl.when(f == 0)
    def _():
        _modulated_norm(x_ref, n2_ref, sc_ref, sh_ref, hbuf, tm)
        o_ref[...] = jnp.zeros((tm, D_MODEL), F32)

    _cast_slices(cast_src, cast_dst, cast_pieces)

    for hh in range(2):
        rows = slice(hh * half, (hh + 1) * half)
        a = jnp.square(jax.nn.relu(_dot(hbuf[rows, :], w1_ref[...]))).astype(BF16)
        o_ref[rows, :] += _dot(a, w2_ref[...])

    @pl.when(f == nf - 1)
    def _():
        def body(c, carry):
            r0 = pl.multiple_of(c * ROWS, ROWS)
            y = x_ref[pl.ds(r0, ROWS), :] + g_ref[...] * o_ref[pl.ds(r0, ROWS), :]
            if final_norm:
                y = _rms_scale(y, fn_ref[...])
            o_ref[pl.ds(r0, ROWS), :] = y
            return carry
        lax.fori_loop(0, tm // ROWS, body, 0)


def _ffn_call(l, x, p, grp, final_norm, cast_from):
    tm, n_tiles, mod_arr, mod_spec = grp["tm"], grp["n_tiles"], grp["mod"], grp["mod_spec"]
    nf = D_FF // TF
    casts = []
    if cast_from is not None:
        casts = [_cast_specs(l + 1, name, cast_from, n_tiles * nf, lambda i, f: i * nf + f)
                 for name in _FFN_CAST]
    kern = functools.partial(_ffn_kernel, tm=tm, nf=nf, final_norm=final_norm,
                             cast_pieces=tuple(c[4] for c in casts))
    outs = pl.pallas_call(
        kern,
        grid=(n_tiles, nf),
        in_specs=[
            pl.BlockSpec((tm, D_MODEL), lambda i, f: (i, 0)),
            pl.BlockSpec((None, 1, D_MODEL), lambda i, f: (l, 0, 0)),
            mod_spec(l, 3), mod_spec(l, 4), mod_spec(l, 5),
            pl.BlockSpec((1, D_MODEL), lambda i, f: (0, 0)),
            pl.BlockSpec((None, D_MODEL, TF), lambda i, f: (f, 0, 0)),
            pl.BlockSpec((TF, D_MODEL), lambda i, f: (f, 0)),
        ] + [c[0] for c in casts],
        out_specs=[pl.BlockSpec((tm, D_MODEL), lambda i, f: (i, 0))] + [c[1] for c in casts],
        out_shape=[jax.ShapeDtypeStruct(x.shape, F32)] + [c[2] for c in casts],
        scratch_shapes=[pltpu.VMEM((tm, D_MODEL), BF16)],
        compiler_params=pltpu.CompilerParams(
            dimension_semantics=("arbitrary", "arbitrary"), vmem_limit_bytes=VMEM_LIMIT),
        name="ffn_" + grp["name"],
    )(x, p["norm2"], mod_arr, mod_arr, mod_arr, p["final_norm"], p["w_ff1"], p["w_ff2"],
      *[c[3] for c in casts])
    return outs[0], outs[1:]


_GATE_CAST = ("w_mix", "w_gate", "w_br_pool", "w_br_conv", "w_br_sgu", "w_out", "w_ff2")
_FFN_CAST = ("w_ff1",)


def kernel(x_prompt, x_sample, state_pool, state_conv, c_prompt, c_sample, norm1, norm2, w_ada, b_ada,
           w_in, w_pool_grp, pool_scale, w_conv, sgu_norm, w_sgu, b_sgu, w_br_pool, w_br_conv,
           w_br_sgu, w_out, w_ff1, w_ff2, final_norm):
    T, nb = DEC_SEQ, DEC_BATCH
    tps = SEQ // TM_PROMPT

    c_all = jnp.concatenate(
        [c_sample, c_prompt, jnp.zeros((ADA_ROWS - nb - BATCH, D_MODEL), F32)], axis=0)
    mod = _ada_call(c_all, w_ada, b_ada)
    modp = mod[:, nb:nb + BATCH].reshape(DEPTH, BATCH, N_MOD, 1, D_MODEL)

    small = {
        "norm1": norm1.reshape(DEPTH, 1, D_MODEL),
        "norm2": norm2.reshape(DEPTH, 1, D_MODEL),
        "final_norm": final_norm.reshape(1, D_MODEL),
        "w_pool_grp": w_pool_grp,
        "pool_scale": pool_scale.reshape(DEPTH, 1, POOL_WIDTH),
        "w_conv": w_conv,
        "sgu_norm": sgu_norm.reshape(DEPTH, 1, SGU_WIDTH),
        "w_sgu": w_sgu,
        "b_sgu_t": jnp.transpose(b_sgu, (0, 2, 1)),
        "w_sgu_v": jnp.repeat(
            jnp.transpose(w_sgu[:, :, :T, :T], (0, 2, 3, 1)).reshape(DEPTH, T * T, 4), GROUP, axis=-1),
        "b_sgu_v": jnp.repeat(jnp.transpose(b_sgu[:, :, :T], (0, 2, 1)), GROUP, axis=-1),
    }
    big = {"w_in": w_in, "w_br_pool": w_br_pool, "w_br_conv": w_br_conv, "w_br_sgu": w_br_sgu,
           "w_out": w_out, "w_ff1": w_ff1, "w_ff2": w_ff2}
    wl = {name: _layout_first_layer(name, big) for name in _GATE_CAST + _FFN_CAST}

    grp_p = {
        "name": "prompt", "tm": TM_PROMPT, "n_tiles": M_PROMPT // TM_PROMPT, "mod": modp,
        "mod_spec": lambda l, k: pl.BlockSpec(
            (None, None, None, 1, D_MODEL), lambda i, j: (l, i // tps, k, 0, 0)),
    }
    grp_s = {
        "name": "sample", "tm": M_SAMPLE, "n_tiles": 1, "mod": mod,
        "mod_spec": lambda l, k: pl.BlockSpec((None, nb, D_MODEL), lambda i, j: (l, 0, k)),
    }

    xp = x_prompt.reshape(M_PROMPT, D_MODEL)
    xs = jnp.transpose(x_sample, (1, 0, 2)).reshape(M_SAMPLE, D_MODEL)
    spool_tm = jnp.transpose(state_pool, (0, 2, 1, 3)).reshape(DEPTH, POOL_BUF * nb, POOL_WIDTH)
    sconv_tm = jnp.transpose(state_conv, (0, 2, 1, 3)).reshape(DEPTH, CONV_BUF * nb, CONV_WIDTH)

    pool_p, conv_p, v_p, pool_s, conv_s, v_s = [], [], [], [], [], []
    last_tile = slice(SEQ // TM_MIX - 1, None, SEQ // TM_MIX)
    for l in range(DEPTH):
        final = l == DEPTH - 1
        p = dict(small, **wl)
        gate_cast = None if final else big
        ffn_cast = None if final else big

        br, ptail, ztail, vtail = _mix_prompt_call(l, xp, modp, p)
        xp, gate_next = _gate_call(l, xp, br, p, grp_p, gate_cast)
        xp, ffn_next = _ffn_call(l, xp, p, grp_p, final, ffn_cast)
        pool_p.append(ptail[last_tile, POOL_HALO - POOL_BUF:, :])
        conv_p.append(ztail[last_tile, CONV_HALO - CONV_BUF:, :])
        v_p.append(vtail.reshape(-1, CHUNK, SGU_WIDTH)[last_tile])

        br, npool, nconv, nv = _mix_sample_call(l, xs, mod, p, spool_tm, sconv_tm)
        xs, _ = _gate_call(l, xs, br, p, grp_s, None)
        xs, _ = _ffn_call(l, xs, p, grp_s, final, None)
        pool_s.append(jnp.transpose(npool.reshape(POOL_BUF, nb, POOL_WIDTH), (1, 0, 2)))
        conv_s.append(jnp.transpose(nconv.reshape(CONV_BUF, nb, CONV_WIDTH), (1, 0, 2)))
        v_s.append(jnp.transpose(nv.reshape(T, nb, SGU_WIDTH), (1, 0, 2)))
        if not final:
            wl = dict(zip(_GATE_CAST + _FFN_CAST, list(gate_next) + list(ffn_next)))

    y_prompt = xp.reshape(BATCH, SEQ, D_MODEL)
    y_sample = jnp.transpose(xs.reshape(T, nb, D_MODEL), (1, 0, 2))
    return (y_prompt, y_sample, jnp.stack(pool_p), jnp.stack(conv_p), jnp.stack(v_p),
            jnp.stack(pool_s), jnp.stack(conv_s), jnp.stack(v_s))
```

```python
import functools

import jax
import jax.numpy as jnp
from jax import lax
from jax.experimental import pallas as pl
from jax.experimental.pallas import tpu as pltpu

F32 = jnp.float32
BF16 = jnp.bfloat16

D_MODEL = 2048
DEPTH = 4
BATCH = 4
SEQ = 2048
DEC_BATCH = 128
DEC_SEQ = 4
PAST_LEN = 16384
POOL_WINDOWS = (2, 4, 8, 16)
GROUP = 128
POOL_WIDTH = 512
POOL_BUF = 15
CONV_WIDTH = 1024
CONV_K = 3
CONV_BUF = 2
CHUNK = 128
SGU_WIDTH = 512
N_MOD = 6
D_FF = 4 * D_MODEL
EPS = 1e-6
OFF_GATE = POOL_WIDTH + 3 * CONV_WIDTH + 2 * SGU_WIDTH
N_IN = OFF_GATE + 3 * D_MODEL

ROWS = 128
CB = 512
POOL_HALO = 16
CONV_HALO = 8
TM_PROMPT = 1024
TM_MIX = 512
M_PROMPT = BATCH * SEQ
M_SAMPLE = DEC_BATCH * DEC_SEQ
TN_GATE = 256
TF = 1024
BF16_SUBLANES = 16
TN_ADA = 1024
ADA_ROWS = 136
VMEM_LIMIT = 60 * 1024 * 1024


def _dot(a, b):
    return jnp.dot(a, b, preferred_element_type=F32)


def _rms_scale(x, g):
    return x * lax.rsqrt(jnp.mean(x * x, axis=-1, keepdims=True) + EPS) * g


def _modulated_norm_slab(x_ref, g_ref, sc_ref, sh_ref, h_ref, rows):
    x = x_ref[rows, :]
    r = lax.rsqrt(jnp.mean(x * x, axis=-1, keepdims=True) + EPS)
    h = x_ref[rows, :] * r * (g_ref[...] * (1.0 + sc_ref[...])) + sh_ref[...]
    h_ref[rows, :] = h.astype(BF16)


def _modulated_norm(x_ref, g_ref, sc_ref, sh_ref, h_ref, tm):
    def body(c, carry):
        rows = pl.ds(pl.multiple_of(c * ROWS, ROWS), ROWS)
        _modulated_norm_slab(x_ref, g_ref, sc_ref, sh_ref, h_ref, rows)
        return carry
    lax.fori_loop(0, tm // ROWS, body, 0)


def _first_middle_last(step_index, n_steps, body):
    pl.when(step_index == 0)(lambda: body(True, False))
    pl.when(jnp.logical_and(step_index > 0, step_index < n_steps - 1))(lambda: body(False, False))
    pl.when(step_index == n_steps - 1)(lambda: body(False, True))


def _ada_kernel(c_ref, w_ref, b_ref, o_ref):
    a = jax.nn.silu(c_ref[...]).astype(BF16)
    o_ref[...] = _dot(a, w_ref[...].astype(BF16)) + b_ref[...]


def _ada_call(c_all, w_ada, b_ada):
    n = N_MOD * D_MODEL
    return pl.pallas_call(
        _ada_kernel,
        grid=(DEPTH, n // TN_ADA),
        in_specs=[
            pl.BlockSpec((ADA_ROWS, D_MODEL), lambda l, j: (0, 0)),
            pl.BlockSpec((None, D_MODEL, TN_ADA), lambda l, j: (l, 0, j)),
            pl.BlockSpec((None, 1, TN_ADA), lambda l, j: (l, 0, j)),
        ],
        out_specs=pl.BlockSpec((None, ADA_ROWS, TN_ADA), lambda l, j: (l, 0, j)),
        out_shape=jax.ShapeDtypeStruct((DEPTH, ADA_ROWS, n), F32),
        compiler_params=pltpu.CompilerParams(
            dimension_semantics=("arbitrary", "arbitrary"), vmem_limit_bytes=VMEM_LIMIT),
        name="ada",
    )(c_all, w_ada, b_ada.reshape(DEPTH, 1, n))


def _mix_prompt_kernel(x_ref, n1_ref, sh_ref, sc_ref, w_ref, wgrp_ref, pscale_ref,
                       wconv_ref, sgun_ref, wsgu_ref, bsgu_ref,
                       br_ref, ptail_ref, ztail_ref, vtail_ref,
                       hbuf, ubuf, vbuf, pbuf, zbuf, bcbuf, *, tm, tiles_per_seq):
    tile_in_seq = pl.program_id(0) % tiles_per_seq
    first_tile = tile_in_seq == 0
    n_slabs = tm // ROWS
    n_cb = CONV_WIDTH // CB
    slab = lambda c: slice(c * ROWS, (c + 1) * ROWS)
    proj = lambda b: _dot(hbuf[...], w_ref[b])

    @pl.when(pl.program_id(0) == 0)
    def _():
        pbuf[tm:tm + POOL_HALO, :] = jnp.zeros((POOL_HALO, CB), F32)
        zbuf[:, tm:tm + CONV_HALO, :] = jnp.zeros((n_cb, CONV_HALO, CB), F32)

    _modulated_norm(x_ref, n1_ref, sc_ref, sh_ref, hbuf, tm)

    pbuf[0:POOL_HALO, :] = jnp.where(first_tile, 0.0, pbuf[tm:tm + POOL_HALO, :])
    for cb in range(n_cb):
        zbuf[cb, 0:CONV_HALO, :] = jnp.where(first_tile, 0.0, zbuf[cb, tm:tm + CONV_HALO, :])

    ubuf[...] = proj(7)
    vbuf[...] = proj(8)
    pbuf[POOL_HALO:POOL_HALO + tm, :] = proj(0)
    ptail_ref[...] = pbuf[tm:tm + POOL_HALO, :]

    row = lax.broadcasted_iota(jnp.int32, (CHUNK, CHUNK), 0)
    col = lax.broadcasted_iota(jnp.int32, (CHUNK, CHUNK), 1)
    n_grp = SGU_WIDTH // GROUP
    wt = [jnp.where(row >= col, wsgu_ref[g], 0.0).astype(BF16) for g in range(n_grp)]
    for c in range(n_slabs):
        vn = _rms_scale(jax.nn.gelu(vbuf[slab(c), :]), sgun_ref[...])
        if c == n_slabs - 1:
            vtail_ref[...] = vn
        for g in range(n_grp):
            cols = slice(g * GROUP, (g + 1) * GROUP)
            sg = _dot(wt[g], vn[:, cols].astype(BF16)) + bsgu_ref[:, g:g + 1]
            out = jax.nn.gelu(ubuf[slab(c), cols]) * sg
            br_ref[slab(c), POOL_WIDTH + CONV_WIDTH + g * GROUP:
                   POOL_WIDTH + CONV_WIDTH + (g + 1) * GROUP] = out.astype(BF16)

    def conv_proj(cb):
        zbuf[cb, CONV_HALO:CONV_HALO + tm, :] = proj(5 + cb) * proj(1 + cb)
        bcbuf[cb] = proj(3 + cb)
        ztail_ref[:, cb * CB:(cb + 1) * CB] = zbuf[cb, tm:tm + CONV_HALO, :]

    def conv_mixer(cb):
        wcols = slice(cb * CB, (cb + 1) * CB)
        for c in range(n_slabs):
            ext = zbuf[cb, c * ROWS:(c + 1) * ROWS + CONV_HALO, :]
            conv = (wconv_ref[0:1, wcols] * pltpu.roll(ext, 2, axis=0)[CONV_HALO:, :]
                    + wconv_ref[1:2, wcols] * pltpu.roll(ext, 1, axis=0)[CONV_HALO:, :]
                    + wconv_ref[2:3, wcols] * ext[CONV_HALO:, :])
            br_ref[slab(c), POOL_WIDTH + cb * CB:POOL_WIDTH + (cb + 1) * CB] = (
                bcbuf[cb, slab(c), :] * conv).astype(BF16)

    conv_proj(0)
    pos0 = tile_in_seq * tm
    for c in range(n_slabs):
        pos = pos0 + c * ROWS + lax.broadcasted_iota(jnp.int32, (ROWS, 1), 0)
        for g, w in enumerate(POOL_WINDOWS):
            cols = slice(g * GROUP, (g + 1) * GROUP)
            ext = pbuf[c * ROWS:(c + 1) * ROWS + POOL_HALO, cols]
            s = ext
            k = 1
            while k < w:
                s = s + pltpu.roll(s, k, axis=0)
                k *= 2
            cnt = jnp.minimum(pos + 1, w).astype(F32)
            d = s[POOL_HALO:, :] / cnt - ext[POOL_HALO:, :]
            y = _dot(d.astype(BF16), wgrp_ref[g].astype(BF16)) * pscale_ref[:, cols]
            br_ref[slab(c), cols] = y.astype(BF16)

    conv_proj(1)
    conv_mixer(0)
    conv_mixer(1)


def _win_index_a(j):
    return jnp.where(j == 3, 7, j)


def _win_index_b(j):
    return jnp.where(j == 3, 8, jnp.maximum(j, 1) + 2)


def _win_index_c(j):
    return jnp.clip(j, 1, 2) + 4


def _mix_prompt_call(l, x, modp, p):
    tm = TM_MIX
    tps = SEQ // tm
    n_tiles = M_PROMPT // tm
    n_cb = CONV_WIDTH // CB

    def mod_spec(k):
        return pl.BlockSpec((None, None, None, 1, D_MODEL), lambda i: (l, i // tps, k, 0, 0))

    full3 = lambda shape: pl.BlockSpec((None,) + shape, lambda i: (l,) + (0,) * len(shape))
    kern = functools.partial(_mix_prompt_kernel, tm=tm, tiles_per_seq=tps)
    return pl.pallas_call(
        kern,
        grid=(n_tiles,),
        in_specs=[
            pl.BlockSpec((tm, D_MODEL), lambda i: (i, 0)),
            full3((1, D_MODEL)),
            mod_spec(0), mod_spec(1),
            pl.BlockSpec((OFF_GATE // CB, D_MODEL, CB), lambda i: (0, 0, 0)),
            full3((4, GROUP, GROUP)),
            full3((1, POOL_WIDTH)),
            full3((CONV_K, CONV_WIDTH)),
            full3((1, SGU_WIDTH)),
            full3((4, CHUNK, CHUNK)),
            full3((CHUNK, 4)),
        ],
        out_specs=[
            pl.BlockSpec((tm, D_MODEL), lambda i: (i, 0)),
            pl.BlockSpec((None, POOL_HALO, POOL_WIDTH), lambda i: (i, 0, 0)),
            pl.BlockSpec((None, CONV_HALO, CONV_WIDTH), lambda i: (i, 0, 0)),
            pl.BlockSpec((CHUNK, SGU_WIDTH), lambda i: (i, 0)),
        ],
        out_shape=[
            jax.ShapeDtypeStruct((M_PROMPT, D_MODEL), BF16),
            jax.ShapeDtypeStruct((n_tiles, POOL_HALO, POOL_WIDTH), F32),
            jax.ShapeDtypeStruct((n_tiles, CONV_HALO, CONV_WIDTH), F32),
            jax.ShapeDtypeStruct((n_tiles * CHUNK, SGU_WIDTH), F32),
        ],
        scratch_shapes=[
            pltpu.VMEM((tm, D_MODEL), BF16),
            pltpu.VMEM((tm, SGU_WIDTH), F32),
            pltpu.VMEM((tm, SGU_WIDTH), F32),
            pltpu.VMEM((tm + POOL_HALO, POOL_WIDTH), F32),
            pltpu.VMEM((n_cb, tm + CONV_HALO, CB), F32),
            pltpu.VMEM((n_cb, tm, CB), F32),
        ],
        compiler_params=pltpu.CompilerParams(
            dimension_semantics=("arbitrary",), vmem_limit_bytes=VMEM_LIMIT),
        name="mix_prompt",
    )(x, p["norm1"], modp, modp, p["w_mix"], p["w_pool_grp"], p["pool_scale"],
      p["w_conv"], p["sgu_norm"], p["w_sgu"], p["b_sgu_t"])


def _mix_sample_kernel(x_ref, n1_ref, sh_ref, sc_ref, wa_ref, wb_ref, wc_ref, wgrp_ref, pscale_ref,
                       wconv_ref, sgun_ref, wv_ref, bv_ref, spool_ref, sconv_ref,
                       br_ref, pool_ref, conv_ref, v_ref,
                       hbuf, t0, t1):
    j = pl.program_id(0)
    nb = DEC_BATCH
    T = DEC_SEQ

    def slab(t):
        return slice(t * nb, (t + 1) * nb)

    @pl.when(j == 0)
    def _():
        _modulated_norm(x_ref, n1_ref, sc_ref, sh_ref, hbuf, T * nb)

    @pl.when(j == 0)
    def _pool():
        t0[...] = _dot(hbuf[...], wa_ref[...])
        keep = POOL_BUF - T
        pool_ref[0:keep * nb, :] = spool_ref[T * nb:POOL_BUF * nb, :]
        pool_ref[keep * nb:POOL_BUF * nb, :] = t0[...]

        def full(s, cols):
            if s < POOL_BUF:
                return spool_ref[slab(s), cols]
            return t0[slab(s - POOL_BUF), cols]

        for t in range(T):
            for g, w in enumerate(POOL_WINDOWS):
                cols = slice(g * GROUP, (g + 1) * GROUP)
                s = full(POOL_BUF + t, cols)
                for k in range(1, w):
                    s = s + full(POOL_BUF + t - k, cols)
                cnt = float(min(PAST_LEN + t + 1, w))
                d = s / cnt - t0[slab(t), cols]
                y = _dot(d.astype(BF16), wgrp_ref[g].astype(BF16)) * pscale_ref[:, cols]
                br_ref[slab(t), cols] = y.astype(BF16)

    @pl.when(jnp.logical_and(j >= 1, j <= 2))
    def _conv():
        h = hbuf[...]
        t1[...] = _dot(h, wc_ref[...]) * _dot(h, wa_ref[...])
        t0[...] = _dot(h, wb_ref[...])

        def full(s):
            if s < CONV_BUF:
                return sconv_ref[slab(s), :]
            return t1[slab(s - CONV_BUF), :]

        for s in range(CONV_BUF):
            conv_ref[slab(s), :] = full(T + s)
        for t in range(T):
            conv = (wconv_ref[0:1, :] * full(t) + wconv_ref[1:2, :] * full(t + 1)
                    + wconv_ref[2:3, :] * full(t + 2))
            br_ref[slab(t), :] = (t0[slab(t), :] * conv).astype(BF16)

    @pl.when(j == 3)
    def _sgu():
        h = hbuf[...]
        t0[...] = _dot(h, wa_ref[...])
        t1[...] = _dot(h, wb_ref[...])
        for t in range(T):
            v_ref[slab(t), :] = _rms_scale(jax.nn.gelu(t1[slab(t), :]), sgun_ref[...])
        for t in range(T):
            sg = bv_ref[t:t + 1, :]
            for k in range(t + 1):
                sg = sg + wv_ref[t * T + k:t * T + k + 1, :] * v_ref[slab(k), :]
            br_ref[slab(t), :] = (jax.nn.gelu(t0[slab(t), :]) * sg).astype(BF16)


def _mix_sample_call(l, x, mod, p, state_pool_tm, state_conv_tm):
    tm = M_SAMPLE

    def mod_spec(k):
        return pl.BlockSpec((None, DEC_BATCH, D_MODEL), lambda j: (l, 0, k))

    full3 = lambda shape: pl.BlockSpec((None,) + shape, lambda j: (l,) + (0,) * len(shape))
    cbi = lambda j: jnp.clip(j - 1, 0, 1)
    return pl.pallas_call(
        _mix_sample_kernel,
        grid=(4,),
        in_specs=[
            pl.BlockSpec((tm, D_MODEL), lambda j: (0, 0)),
            full3((1, D_MODEL)),
            mod_spec(0), mod_spec(1),
            pl.BlockSpec((None, D_MODEL, CB), lambda j: (_win_index_a(j), 0, 0)),
            pl.BlockSpec((None, D_MODEL, CB), lambda j: (_win_index_b(j), 0, 0)),
            pl.BlockSpec((None, D_MODEL, CB), lambda j: (_win_index_c(j), 0, 0)),
            full3((4, GROUP, GROUP)),
            full3((1, POOL_WIDTH)),
            pl.BlockSpec((None, CONV_K, CB), lambda j: (l, 0, cbi(j))),
            full3((1, SGU_WIDTH)),
            full3((DEC_SEQ * DEC_SEQ, SGU_WIDTH)),
            full3((DEC_SEQ, SGU_WIDTH)),
            full3((POOL_BUF * DEC_BATCH, POOL_WIDTH)),
            pl.BlockSpec((None, CONV_BUF * DEC_BATCH, CB), lambda j: (l, 0, cbi(j))),
        ],
        out_specs=[
            pl.BlockSpec((tm, CB), lambda j: (0, j)),
            pl.BlockSpec((POOL_BUF * DEC_BATCH, POOL_WIDTH), lambda j: (0, 0)),
            pl.BlockSpec((CONV_BUF * DEC_BATCH, CB), lambda j: (0, cbi(j))),
            pl.BlockSpec((tm, SGU_WIDTH), lambda j: (0, 0)),
        ],
        out_shape=[
            jax.ShapeDtypeStruct((tm, D_MODEL), BF16),
            jax.ShapeDtypeStruct((POOL_BUF * DEC_BATCH, POOL_WIDTH), F32),
            jax.ShapeDtypeStruct((CONV_BUF * DEC_BATCH, CONV_WIDTH), F32),
            jax.ShapeDtypeStruct((tm, SGU_WIDTH), F32),
        ],
        scratch_shapes=[
            pltpu.VMEM((tm, D_MODEL), BF16),
            pltpu.VMEM((tm, CB), F32),
            pltpu.VMEM((tm, CB), F32),
        ],
        compiler_params=pltpu.CompilerParams(
            dimension_semantics=("arbitrary",), vmem_limit_bytes=VMEM_LIMIT),
        name="mix_sample",
    )(x, p["norm1"], mod, mod, p["w_mix"], p["w_mix"], p["w_mix"], p["w_pool_grp"], p["pool_scale"],
      p["w_conv"], p["sgu_norm"], p["w_sgu_v"], p["b_sgu_v"], state_pool_tm, state_conv_tm)


def _blocked(n_blocks, width, col0=0):
    return [(b, 0, col0 + b * width, width) for b in range(n_blocks)]


_N_GATE_BLOCKS = D_MODEL // TN_GATE
_LAYOUTS = {
    "w_mix": ("w_in", OFF_GATE // CB, CB, _blocked(OFF_GATE // CB, CB)),
    "w_gate": ("w_in", _N_GATE_BLOCKS, 3 * TN_GATE,
               [(j, k * TN_GATE, OFF_GATE + k * D_MODEL + j * TN_GATE, TN_GATE)
                for j in range(_N_GATE_BLOCKS) for k in range(3)]),
    "w_br_pool": ("w_br_pool", _N_GATE_BLOCKS, TN_GATE, _blocked(_N_GATE_BLOCKS, TN_GATE)),
    "w_br_conv": ("w_br_conv", _N_GATE_BLOCKS, TN_GATE, _blocked(_N_GATE_BLOCKS, TN_GATE)),
    "w_br_sgu": ("w_br_sgu", _N_GATE_BLOCKS, TN_GATE, _blocked(_N_GATE_BLOCKS, TN_GATE)),
    "w_out": ("w_out", None, D_MODEL, [(None, 0, 0, D_MODEL)]),
    "w_ff1": ("w_ff1", D_FF // TF, TF, _blocked(D_FF // TF, TF)),
    "w_ff2": ("w_ff2", None, D_MODEL, [(None, 0, 0, D_MODEL)]),
}


def _layout_first_layer(name, big):
    src, n_blocks, width, pieces = _LAYOUTS[name]
    w = big[src][0].astype(BF16)
    rows = w.shape[0]
    if n_blocks is None:
        return w
    if name == "w_gate":
        g = w[:, OFF_GATE:].reshape(rows, 3, n_blocks, TN_GATE)
        return jnp.transpose(g, (2, 0, 1, 3)).reshape(n_blocks, rows, width)
    col0 = pieces[0][2]
    return jnp.transpose(w[:, col0:col0 + n_blocks * width].reshape(rows, n_blocks, width), (1, 0, 2))


def _cast_specs(l_next, name, big, n_steps, step_of):
    src, n_blocks, width, pieces = _LAYOUTS[name]
    _, rows, _ = big[src].shape
    src_cols = max(s0 + wd for (_, _, s0, wd) in pieces)
    rows_per = max(BF16_SUBLANES, rows // n_steps)
    reps = n_steps * rows_per // rows
    in_spec = pl.BlockSpec((None, rows_per, src_cols), lambda i, j: (l_next, step_of(i, j) // reps, 0))
    if n_blocks is None:
        out_spec = pl.BlockSpec((rows_per, width), lambda i, j: (step_of(i, j) // reps, 0))
        shape = (rows, width)
    else:
        out_spec = pl.BlockSpec((n_blocks, rows_per, width), lambda i, j: (0, step_of(i, j) // reps, 0))
        shape = (n_blocks, rows, width)
    return in_spec, out_spec, jax.ShapeDtypeStruct(shape, BF16), big[src], tuple(pieces)


def _cast_slices(src_refs, dst_refs, all_pieces):
    for src, dst, pieces in zip(src_refs, dst_refs, all_pieces):
        for b, d0, s0, wd in pieces:
            v = src[:, s0:s0 + wd].astype(BF16)
            if b is None:
                dst[:, d0:d0 + wd] = v
            else:
                dst[b, :, d0:d0 + wd] = v


def _gate_kernel(*refs, tm, n1, cast_pieces):
    n_fixed = 11
    n_cast = len(cast_pieces)
    (x_ref, br_ref, n1_ref, sh_ref, sc_ref, g_ref, wg_ref,
     wbp_ref, wbc_ref, wbs_ref, wout_ref) = refs[:n_fixed]
    cast_src = refs[n_fixed:n_fixed + n_cast]
    o_ref = refs[n_fixed + n_cast]
    cast_dst = refs[n_fixed + 1 + n_cast:n_fixed + 1 + 2 * n_cast]
    hbuf = refs[n_fixed + 1 + 2 * n_cast]
    half = tm // 2
    slabs = half // ROWS

    def step(first, last):
        _cast_slices(cast_src, cast_dst, cast_pieces)
        for hh in range(2):
            rows = slice(hh * half, (hh + 1) * half)
            if first:
                for c in range(slabs):
                    _modulated_norm_slab(x_ref, n1_ref, sc_ref, sh_ref, hbuf,
                                         slice(hh * half + c * ROWS, hh * half + (c + 1) * ROWS))
            gates = jax.nn.sigmoid(_dot(hbuf[rows, :], wg_ref[...]))
            m = gates[:, 0:TN_GATE] * _dot(br_ref[rows, 0:POOL_WIDTH], wbp_ref[...])
            m = m + (gates[:, TN_GATE:2 * TN_GATE]
                     * _dot(br_ref[rows, POOL_WIDTH:POOL_WIDTH + CONV_WIDTH], wbc_ref[...]))
            m = m + (gates[:, 2 * TN_GATE:]
                     * _dot(br_ref[rows, POOL_WIDTH + CONV_WIDTH:], wbs_ref[...]))
            acc = _dot(m.astype(BF16), wout_ref[...])
            if not first:
                acc = o_ref[rows, :] + acc
            if not last:
                o_ref[rows, :] = acc
                continue
            for c in range(slabs):
                sl = slice(hh * half + c * ROWS, hh * half + (c + 1) * ROWS)
                o_ref[sl, :] = x_ref[sl, :] + g_ref[...] * acc[c * ROWS:(c + 1) * ROWS, :]

    _first_middle_last(pl.program_id(1), n1, step)


def _gate_call(l, x, br, p, grp, cast_from):
    tm, n_tiles, mod_arr, mod_spec = grp["tm"], grp["n_tiles"], grp["mod"], grp["mod_spec"]
    n1 = _N_GATE_BLOCKS

    def branch_spec(width):
        return pl.BlockSpec((None, width, TN_GATE), lambda i, j: (j, 0, 0))

    casts = []
    if cast_from is not None:
        casts = [_cast_specs(l + 1, name, cast_from, n_tiles * n1, lambda i, j: i * n1 + j)
                 for name in _GATE_CAST]
    kern = functools.partial(_gate_kernel, tm=tm, n1=n1, cast_pieces=tuple(c[4] for c in casts))
    outs = pl.pallas_call(
        kern,
        grid=(n_tiles, n1),
        in_specs=[
            pl.BlockSpec((tm, D_MODEL), lambda i, j: (i, 0), pipeline_mode=pl.Buffered(1)),
            pl.BlockSpec((tm, D_MODEL), lambda i, j: (i, 0)),
            pl.BlockSpec((None, 1, D_MODEL), lambda i, j: (l, 0, 0)),
            mod_spec(l, 0), mod_spec(l, 1), mod_spec(l, 2),
            pl.BlockSpec((None, D_MODEL, 3 * TN_GATE), lambda i, j: (j, 0, 0)),
            branch_spec(POOL_WIDTH), branch_spec(CONV_WIDTH), branch_spec(SGU_WIDTH),
            pl.BlockSpec((TN_GATE, D_MODEL), lambda i, j: (j, 0)),
        ] + [c[0] for c in casts],
        out_specs=[pl.BlockSpec((tm, D_MODEL), lambda i, j: (i, 0))] + [c[1] for c in casts],
        out_shape=[jax.ShapeDtypeStruct(x.shape, F32)] + [c[2] for c in casts],
        scratch_shapes=[pltpu.VMEM((tm, D_MODEL), BF16)],
        compiler_params=pltpu.CompilerParams(
            dimension_semantics=("arbitrary", "arbitrary"), vmem_limit_bytes=VMEM_LIMIT),
        name="gate_" + grp["name"],
    )(x, br, p["norm1"], mod_arr, mod_arr, mod_arr, p["w_gate"],
      p["w_br_pool"], p["w_br_conv"], p["w_br_sgu"], p["w_out"], *[c[3] for c in casts])
    return outs[0], outs[1:]


def _ffn_kernel(*refs, tm, nf, final_norm, cast_pieces):
    n_cast = len(cast_pieces)
    x_ref, n2_ref, sh_ref, sc_ref, g_ref, fn_ref, w1_ref, w2_ref = refs[:8]
    cast_src = refs[8:8 + n_cast]
    o_ref = refs[8 + n_cast]
    cast_dst = refs[9 + n_cast:9 + 2 * n_cast]
    hbuf = refs[9 + 2 * n_cast]
    half = tm // 2
    slabs = half // ROWS

    def step(first, last):
        _cast_slices(cast_src, cast_dst, cast_pieces)
        for hh in range(2):
            rows = slice(hh * half, (hh + 1) * half)
            if first:
                for c in range(slabs):
                    _modulated_norm_slab(x_ref, n2_ref, sc_ref, sh_ref, hbuf,
                                         slice(hh * half + c * ROWS, hh * half + (c + 1) * ROWS))
            a = jnp.square(jax.nn.relu(_dot(hbuf[rows, :], w1_ref[...]))).astype(BF16)
            acc = _dot(a, w2_ref[...])
            if not first:
                acc = o_ref[rows, :] + acc
            if not last:
                o_ref[rows, :] = acc
                continue
            for c in range(slabs):
                sl = slice(hh * half + c * ROWS, hh * half + (c + 1) * ROWS)
                y = x_ref[sl, :] + g_ref[...] * acc[c * ROWS:(c + 1) * ROWS, :]
                if final_norm:
                    y = _rms_scale(y, fn_ref[...])
                o_ref[sl, :] = y

    _first_middle_last(pl.program_id(1), nf, step)


def _ffn_call(l, x, p, grp, final_norm, cast_from):
    tm, n_tiles, mod_arr, mod_spec = grp["tm"], grp["n_tiles"], grp["mod"], grp["mod_spec"]
    nf = D_FF // TF
    casts = []
    if cast_from is not None:
        casts = [_cast_specs(l + 1, name, cast_from, n_tiles * nf, lambda i, f: i * nf + f)
                 for name in _FFN_CAST]
    kern = functools.partial(_ffn_kernel, tm=tm, nf=nf, final_norm=final_norm,
                             cast_pieces=tuple(c[4] for c in casts))
    outs = pl.pallas_call(
        kern,
        grid=(n_tiles, nf),
        in_specs=[
            pl.BlockSpec((tm, D_MODEL), lambda i, f: (i, 0)),
            pl.BlockSpec((None, 1, D_MODEL), lambda i, f: (l, 0, 0)),
            mod_spec(l, 3), mod_spec(l, 4), mod_spec(l, 5),
            pl.BlockSpec((1, D_MODEL), lambda i, f: (0, 0)),
            pl.BlockSpec((None, D_MODEL, TF), lambda i, f: (f, 0, 0)),
            pl.BlockSpec((TF, D_MODEL), lambda i, f: (f, 0)),
        ] + [c[0] for c in casts],
        out_specs=[pl.BlockSpec((tm, D_MODEL), lambda i, f: (i, 0))] + [c[1] for c in casts],
        out_shape=[jax.ShapeDtypeStruct(x.shape, F32)] + [c[2] for c in casts],
        scratch_shapes=[pltpu.VMEM((tm, D_MODEL), BF16)],
        compiler_params=pltpu.CompilerParams(
            dimension_semantics=("arbitrary", "arbitrary"), vmem_limit_bytes=VMEM_LIMIT),
        name="ffn_" + grp["name"],
    )(x, p["norm2"], mod_arr, mod_arr, mod_arr, p["final_norm"], p["w_ff1"], p["w_ff2"],
      *[c[3] for c in casts])
    return outs[0], outs[1:]


_GATE_CAST = ("w_mix", "w_gate", "w_br_pool", "w_br_conv", "w_br_sgu", "w_out", "w_ff2")
_FFN_CAST = ("w_ff1",)


def kernel(x_prompt, x_sample, state_pool, state_conv, c_prompt, c_sample, norm1, norm2, w_ada, b_ada,
           w_in, w_pool_grp, pool_scale, w_conv, sgu_norm, w_sgu, b_sgu, w_br_pool, w_br_conv,
           w_br_sgu, w_out, w_ff1, w_ff2, final_norm):
    T, nb = DEC_SEQ, DEC_BATCH
    tps = SEQ // TM_PROMPT

    c_all = jnp.concatenate(
        [c_sample, c_prompt, jnp.zeros((ADA_ROWS - nb - BATCH, D_MODEL), F32)], axis=0)
    mod = _ada_call(c_all, w_ada, b_ada)
    modp = mod[:, nb:nb + BATCH].reshape(DEPTH, BATCH, N_MOD, 1, D_MODEL)

    small = {
        "norm1": norm1.reshape(DEPTH, 1, D_MODEL),
        "norm2": norm2.reshape(DEPTH, 1, D_MODEL),
        "final_norm": final_norm.reshape(1, D_MODEL),
        "w_pool_grp": w_pool_grp,
        "pool_scale": pool_scale.reshape(DEPTH, 1, POOL_WIDTH),
        "w_conv": w_conv,
        "sgu_norm": sgu_norm.reshape(DEPTH, 1, SGU_WIDTH),
        "w_sgu": w_sgu,
        "b_sgu_t": jnp.transpose(b_sgu, (0, 2, 1)),
        "w_sgu_v": jnp.repeat(
            jnp.transpose(w_sgu[:, :, :T, :T], (0, 2, 3, 1)).reshape(DEPTH, T * T, 4), GROUP, axis=-1),
        "b_sgu_v": jnp.repeat(jnp.transpose(b_sgu[:, :, :T], (0, 2, 1)), GROUP, axis=-1),
    }
    big = {"w_in": w_in, "w_br_pool": w_br_pool, "w_br_conv": w_br_conv, "w_br_sgu": w_br_sgu,
           "w_out": w_out, "w_ff1": w_ff1, "w_ff2": w_ff2}
    wl = {name: _layout_first_layer(name, big) for name in _GATE_CAST + _FFN_CAST}

    grp_p = {
        "name": "prompt", "tm": TM_PROMPT, "n_tiles": M_PROMPT // TM_PROMPT, "mod": modp,
        "mod_spec": lambda l, k: pl.BlockSpec(
            (None, None, None, 1, D_MODEL), lambda i, j: (l, i // tps, k, 0, 0)),
    }
    grp_s = {
        "name": "sample", "tm": M_SAMPLE, "n_tiles": 1, "mod": mod,
        "mod_spec": lambda l, k: pl.BlockSpec((None, nb, D_MODEL), lambda i, j: (l, 0, k)),
    }

    xp = x_prompt.reshape(M_PROMPT, D_MODEL)
    xs = jnp.transpose(x_sample, (1, 0, 2)).reshape(M_SAMPLE, D_MODEL)
    spool_tm = jnp.transpose(state_pool, (0, 2, 1, 3)).reshape(DEPTH, POOL_BUF * nb, POOL_WIDTH)
    sconv_tm = jnp.transpose(state_conv, (0, 2, 1, 3)).reshape(DEPTH, CONV_BUF * nb, CONV_WIDTH)

    pool_p, conv_p, v_p, pool_s, conv_s, v_s = [], [], [], [], [], []
    last_tile = slice(SEQ // TM_MIX - 1, None, SEQ // TM_MIX)
    for l in range(DEPTH):
        final = l == DEPTH - 1
        p = dict(small, **wl)
        gate_cast = None if final else big
        ffn_cast = None if final else big

        br, ptail, ztail, vtail = _mix_prompt_call(l, xp, modp, p)
        xp, gate_next = _gate_call(l, xp, br, p, grp_p, gate_cast)
        xp, ffn_next = _ffn_call(l, xp, p, grp_p, final, ffn_cast)
        pool_p.append(ptail[last_tile, POOL_HALO - POOL_BUF:, :])
        conv_p.append(ztail[last_tile, CONV_HALO - CONV_BUF:, :])
        v_p.append(vtail.reshape(-1, CHUNK, SGU_WIDTH)[last_tile])

        br, npool, nconv, nv = _mix_sample_call(l, xs, mod, p, spool_tm, sconv_tm)
        xs, _ = _gate_call(l, xs, br, p, grp_s, None)
        xs, _ = _ffn_call(l, xs, p, grp_s, final, None)
        pool_s.append(jnp.transpose(npool.reshape(POOL_BUF, nb, POOL_WIDTH), (1, 0, 2)))
        conv_s.append(jnp.transpose(nconv.reshape(CONV_BUF, nb, CONV_WIDTH), (1, 0, 2)))
        v_s.append(jnp.transpose(nv.reshape(T, nb, SGU_WIDTH), (1, 0, 2)))
        if not final:
            wl = dict(zip(_GATE_CAST + _FFN_CAST, list(gate_next) + list(ffn_next)))

    y_prompt = xp.reshape(BATCH, SEQ, D_MODEL)
    y_sample = jnp.transpose(xs.reshape(T, nb, D_MODEL), (1, 0, 2))
    return (y_prompt, y_sample, jnp.stack(pool_p), jnp.stack(conv_p), jnp.stack(v_p),
            jnp.stack(pool_s), jnp.stack(conv_s), jnp.stack(v_s))
```

```python
import functools

import jax
import jax.numpy as jnp
from jax import lax
from jax.experimental import pallas as pl
from jax.experimental.pallas import tpu as pltpu

F32 = jnp.float32
BF16 = jnp.bfloat16

D_MODEL = 2048
DEPTH = 4
BATCH = 4
SEQ = 2048
DEC_BATCH = 128
DEC_SEQ = 4
PAST_LEN = 16384
POOL_WINDOWS = (2, 4, 8, 16)
GROUP = 128
POOL_WIDTH = 512
POOL_BUF = 15
CONV_WIDTH = 1024
CONV_K = 3
CONV_BUF = 2
CHUNK = 128
SGU_WIDTH = 512
N_MOD = 6
D_FF = 4 * D_MODEL
EPS = 1e-6
OFF_GATE = POOL_WIDTH + 3 * CONV_WIDTH + 2 * SGU_WIDTH
N_IN = OFF_GATE + 3 * D_MODEL

ROWS = 128
CB = 512
POOL_HALO = 16
CONV_HALO = 8
TM_PROMPT = 1024
TM_MIX = 512
M_PROMPT = BATCH * SEQ
M_SAMPLE = DEC_BATCH * DEC_SEQ
TN_GATE = 256
TF = 1024
BF16_SUBLANES = 16
FIRST_CAST_STEPS = 32
TN_ADA = 1024
ADA_ROWS = 136
VMEM_LIMIT = 60 * 1024 * 1024


def _dot(a, b):
    return jnp.dot(a, b, preferred_element_type=F32)


def _rms_scale(x, g):
    return x * lax.rsqrt(jnp.mean(x * x, axis=-1, keepdims=True) + EPS) * g


def _modulated_norm_slab(x_ref, g_ref, sc_ref, sh_ref, h_ref, rows):
    x = x_ref[rows, :]
    r = lax.rsqrt(jnp.mean(x * x, axis=-1, keepdims=True) + EPS)
    h = x_ref[rows, :] * r * (g_ref[...] * (1.0 + sc_ref[...])) + sh_ref[...]
    h_ref[rows, :] = h.astype(BF16)


def _modulated_norm(x_ref, g_ref, sc_ref, sh_ref, h_ref, tm):
    def body(c, carry):
        rows = pl.ds(pl.multiple_of(c * ROWS, ROWS), ROWS)
        _modulated_norm_slab(x_ref, g_ref, sc_ref, sh_ref, h_ref, rows)
        return carry
    lax.fori_loop(0, tm // ROWS, body, 0)


def _first_middle_last(step_index, n_steps, body):
    pl.when(step_index == 0)(lambda: body(True, False))
    pl.when(jnp.logical_and(step_index > 0, step_index < n_steps - 1))(lambda: body(False, False))
    pl.when(step_index == n_steps - 1)(lambda: body(False, True))


def _ada_kernel(c_ref, w_ref, b_ref, o_ref):
    a = jax.nn.silu(c_ref[...]).astype(BF16)
    o_ref[...] = _dot(a, w_ref[...].astype(BF16)) + b_ref[...]


def _ada_call(c_all, w_ada, b_ada):
    n = N_MOD * D_MODEL
    return pl.pallas_call(
        _ada_kernel,
        grid=(DEPTH, n // TN_ADA),
        in_specs=[
            pl.BlockSpec((ADA_ROWS, D_MODEL), lambda l, j: (0, 0)),
            pl.BlockSpec((None, D_MODEL, TN_ADA), lambda l, j: (l, 0, j)),
            pl.BlockSpec((None, 1, TN_ADA), lambda l, j: (l, 0, j)),
        ],
        out_specs=pl.BlockSpec((None, ADA_ROWS, TN_ADA), lambda l, j: (l, 0, j)),
        out_shape=jax.ShapeDtypeStruct((DEPTH, ADA_ROWS, n), F32),
        compiler_params=pltpu.CompilerParams(
            dimension_semantics=("arbitrary", "arbitrary"), vmem_limit_bytes=VMEM_LIMIT),
        name="ada",
    )(c_all, w_ada, b_ada.reshape(DEPTH, 1, n))


def _mix_prompt_kernel(x_ref, n1_ref, sh_ref, sc_ref, w_ref, wgrp_ref, pscale_ref,
                       wconv_ref, sgun_ref, wsgu_ref, bsgu_ref,
                       br_ref, ptail_ref, ztail_ref, vtail_ref,
                       hbuf, ubuf, vbuf, pbuf, zbuf, bcbuf, *, tm, tiles_per_seq):
    tile_in_seq = pl.program_id(0) % tiles_per_seq
    first_tile = tile_in_seq == 0
    n_slabs = tm // ROWS
    n_cb = CONV_WIDTH // CB
    slab = lambda c: slice(c * ROWS, (c + 1) * ROWS)
    proj = lambda b: _dot(hbuf[...], w_ref[b])

    @pl.when(pl.program_id(0) == 0)
    def _():
        pbuf[tm:tm + POOL_HALO, :] = jnp.zeros((POOL_HALO, CB), F32)
        zbuf[:, tm:tm + CONV_HALO, :] = jnp.zeros((n_cb, CONV_HALO, CB), F32)

    _modulated_norm(x_ref, n1_ref, sc_ref, sh_ref, hbuf, tm)

    pbuf[0:POOL_HALO, :] = jnp.where(first_tile, 0.0, pbuf[tm:tm + POOL_HALO, :])
    for cb in range(n_cb):
        zbuf[cb, 0:CONV_HALO, :] = jnp.where(first_tile, 0.0, zbuf[cb, tm:tm + CONV_HALO, :])

    ubuf[...] = proj(7)
    vbuf[...] = proj(8)
    pbuf[POOL_HALO:POOL_HALO + tm, :] = proj(0)
    ptail_ref[...] = pbuf[tm:tm + POOL_HALO, :]

    row = lax.broadcasted_iota(jnp.int32, (CHUNK, CHUNK), 0)
    col = lax.broadcasted_iota(jnp.int32, (CHUNK, CHUNK), 1)
    n_grp = SGU_WIDTH // GROUP
    wt = [jnp.where(row >= col, wsgu_ref[g], 0.0).astype(BF16) for g in range(n_grp)]
    for c in range(n_slabs):
        vn = _rms_scale(jax.nn.gelu(vbuf[slab(c), :]), sgun_ref[...])
        if c == n_slabs - 1:
            vtail_ref[...] = vn
        for g in range(n_grp):
            cols = slice(g * GROUP, (g + 1) * GROUP)
            sg = _dot(wt[g], vn[:, cols].astype(BF16)) + bsgu_ref[:, g:g + 1]
            out = jax.nn.gelu(ubuf[slab(c), cols]) * sg
            br_ref[slab(c), POOL_WIDTH + CONV_WIDTH + g * GROUP:
                   POOL_WIDTH + CONV_WIDTH + (g + 1) * GROUP] = out.astype(BF16)

    def conv_proj(cb):
        zbuf[cb, CONV_HALO:CONV_HALO + tm, :] = proj(5 + cb) * proj(1 + cb)
        bcbuf[cb] = proj(3 + cb)
        ztail_ref[:, cb * CB:(cb + 1) * CB] = zbuf[cb, tm:tm + CONV_HALO, :]

    def conv_mixer(cb):
        wcols = slice(cb * CB, (cb + 1) * CB)
        for c in range(n_slabs):
            ext = zbuf[cb, c * ROWS:(c + 1) * ROWS + CONV_HALO, :]
            conv = (wconv_ref[0:1, wcols] * pltpu.roll(ext, 2, axis=0)[CONV_HALO:, :]
                    + wconv_ref[1:2, wcols] * pltpu.roll(ext, 1, axis=0)[CONV_HALO:, :]
                    + wconv_ref[2:3, wcols] * ext[CONV_HALO:, :])
            br_ref[slab(c), POOL_WIDTH + cb * CB:POOL_WIDTH + (cb + 1) * CB] = (
                bcbuf[cb, slab(c), :] * conv).astype(BF16)

    conv_proj(0)
    pos0 = tile_in_seq * tm
    for c in range(n_slabs):
        pos = pos0 + c * ROWS + lax.broadcasted_iota(jnp.int32, (ROWS, 1), 0)
        for g, w in enumerate(POOL_WINDOWS):
            cols = slice(g * GROUP, (g + 1) * GROUP)
            ext = pbuf[c * ROWS:(c + 1) * ROWS + POOL_HALO, cols]
            s = ext
            k = 1
            while k < w:
                s = s + pltpu.roll(s, k, axis=0)
                k *= 2
            cnt = jnp.minimum(pos + 1, w).astype(F32)
            d = s[POOL_HALO:, :] / cnt - ext[POOL_HALO:, :]
            y = _dot(d.astype(BF16), wgrp_ref[g].astype(BF16)) * pscale_ref[:, cols]
            br_ref[slab(c), cols] = y.astype(BF16)

    conv_proj(1)
    conv_mixer(0)
    conv_mixer(1)


def _win_index_a(j):
    return jnp.where(j == 3, 7, j)


def _win_index_b(j):
    return jnp.where(j == 3, 8, jnp.maximum(j, 1) + 2)


def _win_index_c(j):
    return jnp.clip(j, 1, 2) + 4


def _mix_prompt_call(l, x, modp, p):
    tm = TM_MIX
    tps = SEQ // tm
    n_tiles = M_PROMPT // tm
    n_cb = CONV_WIDTH // CB

    def mod_spec(k):
        return pl.BlockSpec((None, None, None, 1, D_MODEL), lambda i: (l, i // tps, k, 0, 0))

    full3 = lambda shape: pl.BlockSpec((None,) + shape, lambda i: (l,) + (0,) * len(shape))
    kern = functools.partial(_mix_prompt_kernel, tm=tm, tiles_per_seq=tps)
    return pl.pallas_call(
        kern,
        grid=(n_tiles,),
        in_specs=[
            pl.BlockSpec((tm, D_MODEL), lambda i: (i, 0)),
            full3((1, D_MODEL)),
            mod_spec(0), mod_spec(1),
            pl.BlockSpec((OFF_GATE // CB, D_MODEL, CB), lambda i: (0, 0, 0)),
            full3((4, GROUP, GROUP)),
            full3((1, POOL_WIDTH)),
            full3((CONV_K, CONV_WIDTH)),
            full3((1, SGU_WIDTH)),
            full3((4, CHUNK, CHUNK)),
            full3((CHUNK, 4)),
        ],
        out_specs=[
            pl.BlockSpec((tm, D_MODEL), lambda i: (i, 0)),
            pl.BlockSpec((None, POOL_HALO, POOL_WIDTH), lambda i: (i, 0, 0)),
            pl.BlockSpec((None, CONV_HALO, CONV_WIDTH), lambda i: (i, 0, 0)),
            pl.BlockSpec((CHUNK, SGU_WIDTH), lambda i: (i, 0)),
        ],
        out_shape=[
            jax.ShapeDtypeStruct((M_PROMPT, D_MODEL), BF16),
            jax.ShapeDtypeStruct((n_tiles, POOL_HALO, POOL_WIDTH), F32),
            jax.ShapeDtypeStruct((n_tiles, CONV_HALO, CONV_WIDTH), F32),
            jax.ShapeDtypeStruct((n_tiles * CHUNK, SGU_WIDTH), F32),
        ],
        scratch_shapes=[
            pltpu.VMEM((tm, D_MODEL), BF16),
            pltpu.VMEM((tm, SGU_WIDTH), F32),
            pltpu.VMEM((tm, SGU_WIDTH), F32),
            pltpu.VMEM((tm + POOL_HALO, POOL_WIDTH), F32),
            pltpu.VMEM((n_cb, tm + CONV_HALO, CB), F32),
            pltpu.VMEM((n_cb, tm, CB), F32),
        ],
        compiler_params=pltpu.CompilerParams(
            dimension_semantics=("arbitrary",), vmem_limit_bytes=VMEM_LIMIT),
        name="mix_prompt",
    )(x, p["norm1"], modp, modp, p["w_mix"], p["w_pool_grp"], p["pool_scale"],
      p["w_conv"], p["sgu_norm"], p["w_sgu"], p["b_sgu_t"])


def _mix_sample_kernel(x_ref, n1_ref, sh_ref, sc_ref, wa_ref, wb_ref, wc_ref, wgrp_ref, pscale_ref,
                       wconv_ref, sgun_ref, wv_ref, bv_ref, spool_ref, sconv_ref,
                       br_ref, pool_ref, conv_ref, v_ref,
                       hbuf, t0, t1):
    j = pl.program_id(0)
    nb = DEC_BATCH
    T = DEC_SEQ

    def slab(t):
        return slice(t * nb, (t + 1) * nb)

    @pl.when(j == 0)
    def _():
        _modulated_norm(x_ref, n1_ref, sc_ref, sh_ref, hbuf, T * nb)

    @pl.when(j == 0)
    def _pool():
        t0[...] = _dot(hbuf[...], wa_ref[...])
        keep = POOL_BUF - T
        pool_ref[0:keep * nb, :] = spool_ref[T * nb:POOL_BUF * nb, :]
        pool_ref[keep * nb:POOL_BUF * nb, :] = t0[...]

        def full(s, cols):
            if s < POOL_BUF:
                return spool_ref[slab(s), cols]
            return t0[slab(s - POOL_BUF), cols]

        for t in range(T):
            for g, w in enumerate(POOL_WINDOWS):
                cols = slice(g * GROUP, (g + 1) * GROUP)
                s = full(POOL_BUF + t, cols)
                for k in range(1, w):
                    s = s + full(POOL_BUF + t - k, cols)
                cnt = float(min(PAST_LEN + t + 1, w))
                d = s / cnt - t0[slab(t), cols]
                y = _dot(d.astype(BF16), wgrp_ref[g].astype(BF16)) * pscale_ref[:, cols]
                br_ref[slab(t), cols] = y.astype(BF16)

    @pl.when(jnp.logical_and(j >= 1, j <= 2))
    def _conv():
        h = hbuf[...]
        t1[...] = _dot(h, wc_ref[...]) * _dot(h, wa_ref[...])
        t0[...] = _dot(h, wb_ref[...])

        def full(s):
            if s < CONV_BUF:
                return sconv_ref[slab(s), :]
            return t1[slab(s - CONV_BUF), :]

        for s in range(CONV_BUF):
            conv_ref[slab(s), :] = full(T + s)
        for t in range(T):
            conv = (wconv_ref[0:1, :] * full(t) + wconv_ref[1:2, :] * full(t + 1)
                    + wconv_ref[2:3, :] * full(t + 2))
            br_ref[slab(t), :] = (t0[slab(t), :] * conv).astype(BF16)

    @pl.when(j == 3)
    def _sgu():
        h = hbuf[...]
        t0[...] = _dot(h, wa_ref[...])
        t1[...] = _dot(h, wb_ref[...])
        for t in range(T):
            v_ref[slab(t), :] = _rms_scale(jax.nn.gelu(t1[slab(t), :]), sgun_ref[...])
        for t in range(T):
            sg = bv_ref[t:t + 1, :]
            for k in range(t + 1):
                sg = sg + wv_ref[t * T + k:t * T + k + 1, :] * v_ref[slab(k), :]
            br_ref[slab(t), :] = (jax.nn.gelu(t0[slab(t), :]) * sg).astype(BF16)


def _mix_sample_call(l, x, mod, p, state_pool_tm, state_conv_tm):
    tm = M_SAMPLE

    def mod_spec(k):
        return pl.BlockSpec((None, DEC_BATCH, D_MODEL), lambda j: (l, 0, k))

    full3 = lambda shape: pl.BlockSpec((None,) + shape, lambda j: (l,) + (0,) * len(shape))
    cbi = lambda j: jnp.clip(j - 1, 0, 1)
    return pl.pallas_call(
        _mix_sample_kernel,
        grid=(4,),
        in_specs=[
            pl.BlockSpec((tm, D_MODEL), lambda j: (0, 0)),
            full3((1, D_MODEL)),
            mod_spec(0), mod_spec(1),
            pl.BlockSpec((None, D_MODEL, CB), lambda j: (_win_index_a(j), 0, 0)),
            pl.BlockSpec((None, D_MODEL, CB), lambda j: (_win_index_b(j), 0, 0)),
            pl.BlockSpec((None, D_MODEL, CB), lambda j: (_win_index_c(j), 0, 0)),
            full3((4, GROUP, GROUP)),
            full3((1, POOL_WIDTH)),
            pl.BlockSpec((None, CONV_K, CB), lambda j: (l, 0, cbi(j))),
            full3((1, SGU_WIDTH)),
            full3((DEC_SEQ * DEC_SEQ, SGU_WIDTH)),
            full3((DEC_SEQ, SGU_WIDTH)),
            full3((POOL_BUF * DEC_BATCH, POOL_WIDTH)),
            pl.BlockSpec((None, CONV_BUF * DEC_BATCH, CB), lambda j: (l, 0, cbi(j))),
        ],
        out_specs=[
            pl.BlockSpec((tm, CB), lambda j: (0, j)),
            pl.BlockSpec((POOL_BUF * DEC_BATCH, POOL_WIDTH), lambda j: (0, 0)),
            pl.BlockSpec((CONV_BUF * DEC_BATCH, CB), lambda j: (0, cbi(j))),
            pl.BlockSpec((tm, SGU_WIDTH), lambda j: (0, 0)),
        ],
        out_shape=[
            jax.ShapeDtypeStruct((tm, D_MODEL), BF16),
            jax.ShapeDtypeStruct((POOL_BUF * DEC_BATCH, POOL_WIDTH), F32),
            jax.ShapeDtypeStruct((CONV_BUF * DEC_BATCH, CONV_WIDTH), F32),
            jax.ShapeDtypeStruct((tm, SGU_WIDTH), F32),
        ],
        scratch_shapes=[
            pltpu.VMEM((tm, D_MODEL), BF16),
            pltpu.VMEM((tm, CB), F32),
            pltpu.VMEM((tm, CB), F32),
        ],
        compiler_params=pltpu.CompilerParams(
            dimension_semantics=("arbitrary",), vmem_limit_bytes=VMEM_LIMIT),
        name="mix_sample",
    )(x, p["norm1"], mod, mod, p["w_mix"], p["w_mix"], p["w_mix"], p["w_pool_grp"], p["pool_scale"],
      p["w_conv"], p["sgu_norm"], p["w_sgu_v"], p["b_sgu_v"], state_pool_tm, state_conv_tm)


def _blocked(n_blocks, width, col0=0):
    return [(b, 0, col0 + b * width, width) for b in range(n_blocks)]


_N_GATE_BLOCKS = D_MODEL // TN_GATE
_LAYOUTS = {
    "w_mix": ("w_in", OFF_GATE // CB, CB, _blocked(OFF_GATE // CB, CB)),
    "w_gate": ("w_in", _N_GATE_BLOCKS, 3 * TN_GATE,
               [(j, k * TN_GATE, OFF_GATE + k * D_MODEL + j * TN_GATE, TN_GATE)
                for j in range(_N_GATE_BLOCKS) for k in range(3)]),
    "w_br_pool": ("w_br_pool", _N_GATE_BLOCKS, TN_GATE, _blocked(_N_GATE_BLOCKS, TN_GATE)),
    "w_br_conv": ("w_br_conv", _N_GATE_BLOCKS, TN_GATE, _blocked(_N_GATE_BLOCKS, TN_GATE)),
    "w_br_sgu": ("w_br_sgu", _N_GATE_BLOCKS, TN_GATE, _blocked(_N_GATE_BLOCKS, TN_GATE)),
    "w_out": ("w_out", None, D_MODEL, [(None, 0, 0, D_MODEL)]),
    "w_ff1": ("w_ff1", D_FF // TF, TF, _blocked(D_FF // TF, TF)),
    "w_ff2": ("w_ff2", None, D_MODEL, [(None, 0, 0, D_MODEL)]),
}


def _first_layer_cast_kernel(*refs, cast_pieces):
    n = len(cast_pieces)
    _cast_slices(refs[:n], refs[n:], cast_pieces)


def _first_layer_cast_call(names, big):
    casts = [_cast_specs(0, name, big, FIRST_CAST_STEPS, lambda i, j: i) for name in names]
    outs = pl.pallas_call(
        functools.partial(_first_layer_cast_kernel, cast_pieces=tuple(c[4] for c in casts)),
        grid=(FIRST_CAST_STEPS, 1),
        in_specs=[c[0] for c in casts],
        out_specs=[c[1] for c in casts],
        out_shape=[c[2] for c in casts],
        compiler_params=pltpu.CompilerParams(
            dimension_semantics=("arbitrary", "arbitrary"), vmem_limit_bytes=VMEM_LIMIT),
        name="first_layer_cast",
    )(*[c[3] for c in casts])
    return dict(zip(names, outs))


def _cast_specs(l_next, name, big, n_steps, step_of):
    src, n_blocks, width, pieces = _LAYOUTS[name]
    _, rows, _ = big[src].shape
    src_cols = max(s0 + wd for (_, _, s0, wd) in pieces)
    rows_per = max(BF16_SUBLANES, rows // n_steps)
    reps = n_steps * rows_per // rows
    in_spec = pl.BlockSpec((None, rows_per, src_cols), lambda i, j: (l_next, step_of(i, j) // reps, 0))
    if n_blocks is None:
        out_spec = pl.BlockSpec((rows_per, width), lambda i, j: (step_of(i, j) // reps, 0))
        shape = (rows, width)
    else:
        out_spec = pl.BlockSpec((n_blocks, rows_per, width), lambda i, j: (0, step_of(i, j) // reps, 0))
        shape = (n_blocks, rows, width)
    return in_spec, out_spec, jax.ShapeDtypeStruct(shape, BF16), big[src], tuple(pieces)


def _cast_slices(src_refs, dst_refs, all_pieces):
    for src, dst, pieces in zip(src_refs, dst_refs, all_pieces):
        for b, d0, s0, wd in pieces:
            v = src[:, s0:s0 + wd].astype(BF16)
            if b is None:
                dst[:, d0:d0 + wd] = v
            else:
                dst[b, :, d0:d0 + wd] = v


def _gate_kernel(*refs, tm, n1, cast_pieces):
    n_fixed = 11
    n_cast = len(cast_pieces)
    (x_ref, br_ref, n1_ref, sh_ref, sc_ref, g_ref, wg_ref,
     wbp_ref, wbc_ref, wbs_ref, wout_ref) = refs[:n_fixed]
    cast_src = refs[n_fixed:n_fixed + n_cast]
    o_ref = refs[n_fixed + n_cast]
    cast_dst = refs[n_fixed + 1 + n_cast:n_fixed + 1 + 2 * n_cast]
    hbuf = refs[n_fixed + 1 + 2 * n_cast]
    half = tm // 2
    slabs = half // ROWS

    def step(first, last):
        _cast_slices(cast_src, cast_dst, cast_pieces)
        for hh in range(2):
            rows = slice(hh * half, (hh + 1) * half)
            if first:
                for c in range(slabs):
                    _modulated_norm_slab(x_ref, n1_ref, sc_ref, sh_ref, hbuf,
                                         slice(hh * half + c * ROWS, hh * half + (c + 1) * ROWS))
            gates = jax.nn.sigmoid(_dot(hbuf[rows, :], wg_ref[...]))
            m = gates[:, 0:TN_GATE] * _dot(br_ref[rows, 0:POOL_WIDTH], wbp_ref[...])
            m = m + (gates[:, TN_GATE:2 * TN_GATE]
                     * _dot(br_ref[rows, POOL_WIDTH:POOL_WIDTH + CONV_WIDTH], wbc_ref[...]))
            m = m + (gates[:, 2 * TN_GATE:]
                     * _dot(br_ref[rows, POOL_WIDTH + CONV_WIDTH:], wbs_ref[...]))
            acc = _dot(m.astype(BF16), wout_ref[...])
            if not first:
                acc = o_ref[rows, :] + acc
            if not last:
                o_ref[rows, :] = acc
                continue
            for c in range(slabs):
                sl = slice(hh * half + c * ROWS, hh * half + (c + 1) * ROWS)
                o_ref[sl, :] = x_ref[sl, :] + g_ref[...] * acc[c * ROWS:(c + 1) * ROWS, :]

    _first_middle_last(pl.program_id(1), n1, step)


def _gate_call(l, x, br, p, grp, cast_from):
    tm, n_tiles, mod_arr, mod_spec = grp["tm"], grp["n_tiles"], grp["mod"], grp["mod_spec"]
    n1 = _N_GATE_BLOCKS

    def branch_spec(width):
        return pl.BlockSpec((None, width, TN_GATE), lambda i, j: (j, 0, 0))

    casts = []
    if cast_from is not None:
        casts = [_cast_specs(l + 1, name, cast_from, n_tiles * n1, lambda i, j: i * n1 + j)
                 for name in _GATE_CAST]
    kern = functools.partial(_gate_kernel, tm=tm, n1=n1, cast_pieces=tuple(c[4] for c in casts))
    outs = pl.pallas_call(
        kern,
        grid=(n_tiles, n1),
        in_specs=[
            pl.BlockSpec((tm, D_MODEL), lambda i, j: (i, 0), pipeline_mode=pl.Buffered(1)),
            pl.BlockSpec((tm, D_MODEL), lambda i, j: (i, 0)),
            pl.BlockSpec((None, 1, D_MODEL), lambda i, j: (l, 0, 0)),
            mod_spec(l, 0), mod_spec(l, 1), mod_spec(l, 2),
            pl.BlockSpec((None, D_MODEL, 3 * TN_GATE), lambda i, j: (j, 0, 0)),
            branch_spec(POOL_WIDTH), branch_spec(CONV_WIDTH), branch_spec(SGU_WIDTH),
            pl.BlockSpec((TN_GATE, D_MODEL), lambda i, j: (j, 0)),
        ] + [c[0] for c in casts],
        out_specs=[pl.BlockSpec((tm, D_MODEL), lambda i, j: (i, 0))] + [c[1] for c in casts],
        out_shape=[jax.ShapeDtypeStruct(x.shape, F32)] + [c[2] for c in casts],
        scratch_shapes=[pltpu.VMEM((tm, D_MODEL), BF16)],
        compiler_params=pltpu.CompilerParams(
            dimension_semantics=("arbitrary", "arbitrary"), vmem_limit_bytes=VMEM_LIMIT),
        name="gate_" + grp["name"],
    )(x, br, p["norm1"], mod_arr, mod_arr, mod_arr, p["w_gate"],
      p["w_br_pool"], p["w_br_conv"], p["w_br_sgu"], p["w_out"], *[c[3] for c in casts])
    return outs[0], outs[1:]


def _ffn_kernel(*refs, tm, nf, final_norm, cast_pieces):
    n_cast = len(cast_pieces)
    x_ref, n2_ref, sh_ref, sc_ref, g_ref, fn_ref, w1_ref, w2_ref = refs[:8]
    cast_src = refs[8:8 + n_cast]
    o_ref = refs[8 + n_cast]
    cast_dst = refs[9 + n_cast:9 + 2 * n_cast]
    hbuf = refs[9 + 2 * n_cast]
    half = tm // 2
    slabs = half // ROWS

    def step(first, last):
        _cast_slices(cast_src, cast_dst, cast_pieces)
        for hh in range(2):
            rows = slice(hh * half, (hh + 1) * half)
            if first:
                for c in range(slabs):
                    _modulated_norm_slab(x_ref, n2_ref, sc_ref, sh_ref, hbuf,
                                         slice(hh * half + c * ROWS, hh * half + (c + 1) * ROWS))
            a = jnp.square(jax.nn.relu(_dot(hbuf[rows, :], w1_ref[...]))).astype(BF16)
            acc = _dot(a, w2_ref[...])
            if not first:
                acc = o_ref[rows, :] + acc
            if not last:
                o_ref[rows, :] = acc
                continue
            for c in range(slabs):
                sl = slice(hh * half + c * ROWS, hh * half + (c + 1) * ROWS)
                y = x_ref[sl, :] + g_ref[...] * acc[c * ROWS:(c + 1) * ROWS, :]
                if final_norm:
                    y = _rms_scale(y, fn_ref[...])
                o_ref[sl, :] = y

    _first_middle_last(pl.program_id(1), nf, step)


def _ffn_call(l, x, p, grp, final_norm, cast_from):
    tm, n_tiles, mod_arr, mod_spec = grp["tm"], grp["n_tiles"], grp["mod"], grp["mod_spec"]
    nf = D_FF // TF
    casts = []
    if cast_from is not None:
        casts = [_cast_specs(l + 1, name, cast_from, n_tiles * nf, lambda i, f: i * nf + f)
                 for name in _FFN_CAST]
    kern = functools.partial(_ffn_kernel, tm=tm, nf=nf, final_norm=final_norm,
                             cast_pieces=tuple(c[4] for c in casts))
    outs = pl.pallas_call(
        kern,
        grid=(n_tiles, nf),
        in_specs=[
            pl.BlockSpec((tm, D_MODEL), lambda i, f: (i, 0)),
            pl.BlockSpec((None, 1, D_MODEL), lambda i, f: (l, 0, 0)),
            mod_spec(l, 3), mod_spec(l, 4), mod_spec(l, 5),
            pl.BlockSpec((1, D_MODEL), lambda i, f: (0, 0)),
            pl.BlockSpec((None, D_MODEL, TF), lambda i, f: (f, 0, 0)),
            pl.BlockSpec((TF, D_MODEL), lambda i, f: (f, 0)),
        ] + [c[0] for c in casts],
        out_specs=[pl.BlockSpec((tm, D_MODEL), lambda i, f: (i, 0))] + [c[1] for c in casts],
        out_shape=[jax.ShapeDtypeStruct(x.shape, F32)] + [c[2] for c in casts],
        scratch_shapes=[pltpu.VMEM((tm, D_MODEL), BF16)],
        compiler_params=pltpu.CompilerParams(
            dimension_semantics=("arbitrary", "arbitrary"), vmem_limit_bytes=VMEM_LIMIT),
        name="ffn_" + grp["name"],
    )(x, p["norm2"], mod_arr, mod_arr, mod_arr, p["final_norm"], p["w_ff1"], p["w_ff2"],
      *[c[3] for c in casts])
    return outs[0], outs[1:]


_GATE_CAST = ("w_mix", "w_gate", "w_br_pool", "w_br_conv", "w_br_sgu", "w_out", "w_ff2")
_FFN_CAST = ("w_ff1",)


def kernel(x_prompt, x_sample, state_pool, state_conv, c_prompt, c_sample, norm1, norm2, w_ada, b_ada,
           w_in, w_pool_grp, pool_scale, w_conv, sgu_norm, w_sgu, b_sgu, w_br_pool, w_br_conv,
           w_br_sgu, w_out, w_ff1, w_ff2, final_norm):
    T, nb = DEC_SEQ, DEC_BATCH
    tps = SEQ // TM_PROMPT

    c_all = jnp.concatenate(
        [c_sample, c_prompt, jnp.zeros((ADA_ROWS - nb - BATCH, D_MODEL), F32)], axis=0)
    mod = _ada_call(c_all, w_ada, b_ada)
    modp = mod[:, nb:nb + BATCH].reshape(DEPTH, BATCH, N_MOD, 1, D_MODEL)

    small = {
        "norm1": norm1.reshape(DEPTH, 1, D_MODEL),
        "norm2": norm2.reshape(DEPTH, 1, D_MODEL),
        "final_norm": final_norm.reshape(1, D_MODEL),
        "w_pool_grp": w_pool_grp,
        "pool_scale": pool_scale.reshape(DEPTH, 1, POOL_WIDTH),
        "w_conv": w_conv,
        "sgu_norm": sgu_norm.reshape(DEPTH, 1, SGU_WIDTH),
        "w_sgu": w_sgu,
        "b_sgu_t": jnp.transpose(b_sgu, (0, 2, 1)),
        "w_sgu_v": jnp.repeat(
            jnp.transpose(w_sgu[:, :, :T, :T], (0, 2, 3, 1)).reshape(DEPTH, T * T, 4), GROUP, axis=-1),
        "b_sgu_v": jnp.repeat(jnp.transpose(b_sgu[:, :, :T], (0, 2, 1)), GROUP, axis=-1),
    }
    big = {"w_in": w_in, "w_br_pool": w_br_pool, "w_br_conv": w_br_conv, "w_br_sgu": w_br_sgu,
           "w_out": w_out, "w_ff1": w_ff1, "w_ff2": w_ff2}
    wl = _first_layer_cast_call(_GATE_CAST + _FFN_CAST, big)

    grp_p = {
        "name": "prompt", "tm": TM_PROMPT, "n_tiles": M_PROMPT // TM_PROMPT, "mod": modp,
        "mod_spec": lambda l, k: pl.BlockSpec(
            (None, None, None, 1, D_MODEL), lambda i, j: (l, i // tps, k, 0, 0)),
    }
    grp_s = {
        "name": "sample", "tm": M_SAMPLE, "n_tiles": 1, "mod": mod,
        "mod_spec": lambda l, k: pl.BlockSpec((None, nb, D_MODEL), lambda i, j: (l, 0, k)),
    }

    xp = x_prompt.reshape(M_PROMPT, D_MODEL)
    xs = jnp.transpose(x_sample, (1, 0, 2)).reshape(M_SAMPLE, D_MODEL)
    spool_tm = jnp.transpose(state_pool, (0, 2, 1, 3)).reshape(DEPTH, POOL_BUF * nb, POOL_WIDTH)
    sconv_tm = jnp.transpose(state_conv, (0, 2, 1, 3)).reshape(DEPTH, CONV_BUF * nb, CONV_WIDTH)

    pool_p, conv_p, v_p, pool_s, conv_s, v_s = [], [], [], [], [], []
    last_tile = slice(SEQ // TM_MIX - 1, None, SEQ // TM_MIX)
    for l in range(DEPTH):
        final = l == DEPTH - 1
        p = dict(small, **wl)
        gate_cast = None if final else big
        ffn_cast = None if final else big

        br, ptail, ztail, vtail = _mix_prompt_call(l, xp, modp, p)
        xp, gate_next = _gate_call(l, xp, br, p, grp_p, gate_cast)
        xp, ffn_next = _ffn_call(l, xp, p, grp_p, final, ffn_cast)
        pool_p.append(ptail[last_tile, POOL_HALO - POOL_BUF:, :])
        conv_p.append(ztail[last_tile, CONV_HALO - CONV_BUF:, :])
        v_p.append(vtail.reshape(-1, CHUNK, SGU_WIDTH)[last_tile])

        br, npool, nconv, nv = _mix_sample_call(l, xs, mod, p, spool_tm, sconv_tm)
        xs, _ = _gate_call(l, xs, br, p, grp_s, None)
        xs, _ = _ffn_call(l, xs, p, grp_s, final, None)
        pool_s.append(jnp.transpose(npool.reshape(POOL_BUF, nb, POOL_WIDTH), (1, 0, 2)))
        conv_s.append(jnp.transpose(nconv.reshape(CONV_BUF, nb, CONV_WIDTH), (1, 0, 2)))
        v_s.append(jnp.transpose(nv.reshape(T, nb, SGU_WIDTH), (1, 0, 2)))
        if not final:
            wl = dict(zip(_GATE_CAST + _FFN_CAST, list(gate_next) + list(ffn_next)))

    y_prompt = xp.reshape(BATCH, SEQ, D_MODEL)
    y_sample = jnp.transpose(xs.reshape(T, nb, D_MODEL), (1, 0, 2))
    return (y_prompt, y_sample, jnp.stack(pool_p), jnp.stack(conv_p), jnp.stack(v_p),
            jnp.stack(pool_s), jnp.stack(conv_s), jnp.stack(v_s))
```

```python
import functools

import jax
import jax.numpy as jnp
from jax import lax
from jax.experimental import pallas as pl
from jax.experimental.pallas import tpu as pltpu

F32 = jnp.float32
BF16 = jnp.bfloat16

D_MODEL = 2048
DEPTH = 4
BATCH = 4
SEQ = 2048
DEC_BATCH = 128
DEC_SEQ = 4
PAST_LEN = 16384
POOL_WINDOWS = (2, 4, 8, 16)
GROUP = 128
POOL_WIDTH = 512
POOL_BUF = 15
CONV_WIDTH = 1024
CONV_K = 3
CONV_BUF = 2
CHUNK = 128
SGU_WIDTH = 512
N_MOD = 6
D_FF = 4 * D_MODEL
EPS = 1e-6
OFF_GATE = POOL_WIDTH + 3 * CONV_WIDTH + 2 * SGU_WIDTH
N_IN = OFF_GATE + 3 * D_MODEL

ROWS = 128
CB = 512
POOL_HALO = 16
CONV_HALO = 8
TM_PROMPT = 1024
TM_MIX = 512
M_PROMPT = BATCH * SEQ
M_SAMPLE = DEC_BATCH * DEC_SEQ
TN_GATE = 256
TF = 1024
BF16_SUBLANES = 16
FIRST_CAST_STEPS = 32
TN_ADA = 1024
ADA_ROWS = 136
VMEM_LIMIT = 60 * 1024 * 1024


def _dot(a, b):
    return jnp.dot(a, b, preferred_element_type=F32)


def _rms_scale(x, g):
    return x * lax.rsqrt(jnp.mean(x * x, axis=-1, keepdims=True) + EPS) * g


def _modulated_norm_slab(x_ref, g_ref, sc_ref, sh_ref, h_ref, rows):
    x = x_ref[rows, :]
    r = lax.rsqrt(jnp.mean(x * x, axis=-1, keepdims=True) + EPS)
    h = x_ref[rows, :] * r * (g_ref[...] * (1.0 + sc_ref[...])) + sh_ref[...]
    h_ref[rows, :] = h.astype(BF16)


def _modulated_norm(x_ref, g_ref, sc_ref, sh_ref, h_ref, tm):
    def body(c, carry):
        rows = pl.ds(pl.multiple_of(c * ROWS, ROWS), ROWS)
        _modulated_norm_slab(x_ref, g_ref, sc_ref, sh_ref, h_ref, rows)
        return carry
    lax.fori_loop(0, tm // ROWS, body, 0)


def _first_middle_last(step_index, n_steps, body):
    pl.when(step_index == 0)(lambda: body(True, False))
    pl.when(jnp.logical_and(step_index > 0, step_index < n_steps - 1))(lambda: body(False, False))
    pl.when(step_index == n_steps - 1)(lambda: body(False, True))


def _ada_kernel(c_ref, w_ref, b_ref, o_ref):
    a = jax.nn.silu(c_ref[...]).astype(BF16)
    o_ref[...] = _dot(a, w_ref[...].astype(BF16)) + b_ref[...]


def _ada_call(c_all, w_ada, b_ada):
    n = N_MOD * D_MODEL
    return pl.pallas_call(
        _ada_kernel,
        grid=(DEPTH, n // TN_ADA),
        in_specs=[
            pl.BlockSpec((ADA_ROWS, D_MODEL), lambda l, j: (0, 0)),
            pl.BlockSpec((None, D_MODEL, TN_ADA), lambda l, j: (l, 0, j)),
            pl.BlockSpec((None, 1, TN_ADA), lambda l, j: (l, 0, j)),
        ],
        out_specs=pl.BlockSpec((None, ADA_ROWS, TN_ADA), lambda l, j: (l, 0, j)),
        out_shape=jax.ShapeDtypeStruct((DEPTH, ADA_ROWS, n), F32),
        compiler_params=pltpu.CompilerParams(
            dimension_semantics=("arbitrary", "arbitrary"), vmem_limit_bytes=VMEM_LIMIT),
        name="ada",
    )(c_all, w_ada, b_ada.reshape(DEPTH, 1, n))


def _mix_prompt_kernel(x_ref, n1_ref, sh_ref, sc_ref, w_ref, wgrp_ref, pscale_ref,
                       wconv_ref, sgun_ref, wsgu_ref, bsgu_ref, cast_src_ref,
                       br_ref, ptail_ref, ztail_ref, vtail_ref, cast_dst_ref,
                       hbuf, ubuf, vbuf, pbuf, zbuf, bcbuf, *, tm, tiles_per_seq, cast_pieces):
    tile_in_seq = pl.program_id(0) % tiles_per_seq
    first_tile = tile_in_seq == 0
    n_slabs = tm // ROWS
    n_cb = CONV_WIDTH // CB
    slab = lambda c: slice(c * ROWS, (c + 1) * ROWS)
    proj = lambda b: _dot(hbuf[...], w_ref[b])

    @pl.when(pl.program_id(0) == 0)
    def _():
        pbuf[tm:tm + POOL_HALO, :] = jnp.zeros((POOL_HALO, CB), F32)
        zbuf[:, tm:tm + CONV_HALO, :] = jnp.zeros((n_cb, CONV_HALO, CB), F32)

    _modulated_norm(x_ref, n1_ref, sc_ref, sh_ref, hbuf, tm)

    _cast_slices([cast_src_ref], [cast_dst_ref], [cast_pieces])

    pbuf[0:POOL_HALO, :] = jnp.where(first_tile, 0.0, pbuf[tm:tm + POOL_HALO, :])
    for cb in range(n_cb):
        zbuf[cb, 0:CONV_HALO, :] = jnp.where(first_tile, 0.0, zbuf[cb, tm:tm + CONV_HALO, :])

    ubuf[...] = proj(7)
    vbuf[...] = proj(8)
    pbuf[POOL_HALO:POOL_HALO + tm, :] = proj(0)
    ptail_ref[...] = pbuf[tm:tm + POOL_HALO, :]

    row = lax.broadcasted_iota(jnp.int32, (CHUNK, CHUNK), 0)
    col = lax.broadcasted_iota(jnp.int32, (CHUNK, CHUNK), 1)
    n_grp = SGU_WIDTH // GROUP
    wt = [jnp.where(row >= col, wsgu_ref[g], 0.0).astype(BF16) for g in range(n_grp)]
    for c in range(n_slabs):
        vn = _rms_scale(jax.nn.gelu(vbuf[slab(c), :]), sgun_ref[...])
        if c == n_slabs - 1:
            vtail_ref[...] = vn
        for g in range(n_grp):
            cols = slice(g * GROUP, (g + 1) * GROUP)
            sg = _dot(wt[g], vn[:, cols].astype(BF16)) + bsgu_ref[:, g:g + 1]
            out = jax.nn.gelu(ubuf[slab(c), cols]) * sg
            br_ref[slab(c), POOL_WIDTH + CONV_WIDTH + g * GROUP:
                   POOL_WIDTH + CONV_WIDTH + (g + 1) * GROUP] = out.astype(BF16)

    def conv_proj(cb):
        zbuf[cb, CONV_HALO:CONV_HALO + tm, :] = proj(5 + cb) * proj(1 + cb)
        bcbuf[cb] = proj(3 + cb)
        ztail_ref[:, cb * CB:(cb + 1) * CB] = zbuf[cb, tm:tm + CONV_HALO, :]

    def conv_mixer(cb):
        wcols = slice(cb * CB, (cb + 1) * CB)
        for c in range(n_slabs):
            ext = zbuf[cb, c * ROWS:(c + 1) * ROWS + CONV_HALO, :]
            conv = (wconv_ref[0:1, wcols] * pltpu.roll(ext, 2, axis=0)[CONV_HALO:, :]
                    + wconv_ref[1:2, wcols] * pltpu.roll(ext, 1, axis=0)[CONV_HALO:, :]
                    + wconv_ref[2:3, wcols] * ext[CONV_HALO:, :])
            br_ref[slab(c), POOL_WIDTH + cb * CB:POOL_WIDTH + (cb + 1) * CB] = (
                bcbuf[cb, slab(c), :] * conv).astype(BF16)

    conv_proj(0)
    pos0 = tile_in_seq * tm
    for c in range(n_slabs):
        pos = pos0 + c * ROWS + lax.broadcasted_iota(jnp.int32, (ROWS, 1), 0)
        for g, w in enumerate(POOL_WINDOWS):
            cols = slice(g * GROUP, (g + 1) * GROUP)
            ext = pbuf[c * ROWS:(c + 1) * ROWS + POOL_HALO, cols]
            s = ext
            k = 1
            while k < w:
                s = s + pltpu.roll(s, k, axis=0)
                k *= 2
            cnt = jnp.minimum(pos + 1, w).astype(F32)
            d = s[POOL_HALO:, :] / cnt - ext[POOL_HALO:, :]
            y = _dot(d.astype(BF16), wgrp_ref[g].astype(BF16)) * pscale_ref[:, cols]
            br_ref[slab(c), cols] = y.astype(BF16)

    conv_proj(1)
    conv_mixer(0)
    conv_mixer(1)


def _win_index_a(j):
    return jnp.where(j == 3, 7, j)


def _win_index_b(j):
    return jnp.where(j == 3, 8, jnp.maximum(j, 1) + 2)


def _win_index_c(j):
    return jnp.clip(j, 1, 2) + 4


def _mix_prompt_call(l, x, modp, p, big):
    tm = TM_MIX
    tps = SEQ // tm
    n_tiles = M_PROMPT // tm
    n_cb = CONV_WIDTH // CB

    def mod_spec(k):
        return pl.BlockSpec((None, None, None, 1, D_MODEL), lambda i: (l, i // tps, k, 0, 0))

    full3 = lambda shape: pl.BlockSpec((None,) + shape, lambda i: (l,) + (0,) * len(shape))
    cast = _cast_specs(l, "w_ff2", big, n_tiles, lambda i: i)
    kern = functools.partial(_mix_prompt_kernel, tm=tm, tiles_per_seq=tps, cast_pieces=cast[4])
    return pl.pallas_call(
        kern,
        grid=(n_tiles,),
        in_specs=[
            pl.BlockSpec((tm, D_MODEL), lambda i: (i, 0)),
            full3((1, D_MODEL)),
            mod_spec(0), mod_spec(1),
            pl.BlockSpec((OFF_GATE // CB, D_MODEL, CB), lambda i: (0, 0, 0)),
            full3((4, GROUP, GROUP)),
            full3((1, POOL_WIDTH)),
            full3((CONV_K, CONV_WIDTH)),
            full3((1, SGU_WIDTH)),
            full3((4, CHUNK, CHUNK)),
            full3((CHUNK, 4)),
            cast[0],
        ],
        out_specs=[
            pl.BlockSpec((tm, D_MODEL), lambda i: (i, 0)),
            pl.BlockSpec((None, POOL_HALO, POOL_WIDTH), lambda i: (i, 0, 0)),
            pl.BlockSpec((None, CONV_HALO, CONV_WIDTH), lambda i: (i, 0, 0)),
            pl.BlockSpec((CHUNK, SGU_WIDTH), lambda i: (i, 0)),
            cast[1],
        ],
        out_shape=[
            jax.ShapeDtypeStruct((M_PROMPT, D_MODEL), BF16),
            jax.ShapeDtypeStruct((n_tiles, POOL_HALO, POOL_WIDTH), F32),
            jax.ShapeDtypeStruct((n_tiles, CONV_HALO, CONV_WIDTH), F32),
            jax.ShapeDtypeStruct((n_tiles * CHUNK, SGU_WIDTH), F32),
            cast[2],
        ],
        scratch_shapes=[
            pltpu.VMEM((tm, D_MODEL), BF16),
            pltpu.VMEM((tm, SGU_WIDTH), F32),
            pltpu.VMEM((tm, SGU_WIDTH), F32),
            pltpu.VMEM((tm + POOL_HALO, POOL_WIDTH), F32),
            pltpu.VMEM((n_cb, tm + CONV_HALO, CB), F32),
            pltpu.VMEM((n_cb, tm, CB), F32),
        ],
        compiler_params=pltpu.CompilerParams(
            dimension_semantics=("arbitrary",), vmem_limit_bytes=VMEM_LIMIT),
        name="mix_prompt",
    )(x, p["norm1"], modp, modp, p["w_mix"], p["w_pool_grp"], p["pool_scale"],
      p["w_conv"], p["sgu_norm"], p["w_sgu"], p["b_sgu_t"], cast[3])


def _mix_sample_kernel(x_ref, n1_ref, sh_ref, sc_ref, wa_ref, wb_ref, wc_ref, wgrp_ref, pscale_ref,
                       wconv_ref, sgun_ref, wv_ref, bv_ref, spool_ref, sconv_ref,
                       br_ref, pool_ref, conv_ref, v_ref,
                       hbuf, t0, t1):
    j = pl.program_id(0)
    nb = DEC_BATCH
    T = DEC_SEQ

    def slab(t):
        return slice(t * nb, (t + 1) * nb)

    @pl.when(j == 0)
    def _():
        _modulated_norm(x_ref, n1_ref, sc_ref, sh_ref, hbuf, T * nb)

    @pl.when(j == 0)
    def _pool():
        t0[...] = _dot(hbuf[...], wa_ref[...])
        keep = POOL_BUF - T
        pool_ref[0:keep * nb, :] = spool_ref[T * nb:POOL_BUF * nb, :]
        pool_ref[keep * nb:POOL_BUF * nb, :] = t0[...]

        def full(s, cols):
            if s < POOL_BUF:
                return spool_ref[slab(s), cols]
            return t0[slab(s - POOL_BUF), cols]

        for t in range(T):
            for g, w in enumerate(POOL_WINDOWS):
                cols = slice(g * GROUP, (g + 1) * GROUP)
                s = full(POOL_BUF + t, cols)
                for k in range(1, w):
                    s = s + full(POOL_BUF + t - k, cols)
                cnt = float(min(PAST_LEN + t + 1, w))
                d = s / cnt - t0[slab(t), cols]
                y = _dot(d.astype(BF16), wgrp_ref[g].astype(BF16)) * pscale_ref[:, cols]
                br_ref[slab(t), cols] = y.astype(BF16)

    @pl.when(jnp.logical_and(j >= 1, j <= 2))
    def _conv():
        h = hbuf[...]
        t1[...] = _dot(h, wc_ref[...]) * _dot(h, wa_ref[...])
        t0[...] = _dot(h, wb_ref[...])

        def full(s):
            if s < CONV_BUF:
                return sconv_ref[slab(s), :]
            return t1[slab(s - CONV_BUF), :]

        for s in range(CONV_BUF):
            conv_ref[slab(s), :] = full(T + s)
        for t in range(T):
            conv = (wconv_ref[0:1, :] * full(t) + wconv_ref[1:2, :] * full(t + 1)
                    + wconv_ref[2:3, :] * full(t + 2))
            br_ref[slab(t), :] = (t0[slab(t), :] * conv).astype(BF16)

    @pl.when(j == 3)
    def _sgu():
        h = hbuf[...]
        t0[...] = _dot(h, wa_ref[...])
        t1[...] = _dot(h, wb_ref[...])
        for t in range(T):
            v_ref[slab(t), :] = _rms_scale(jax.nn.gelu(t1[slab(t), :]), sgun_ref[...])
        for t in range(T):
            sg = bv_ref[t:t + 1, :]
            for k in range(t + 1):
                sg = sg + wv_ref[t * T + k:t * T + k + 1, :] * v_ref[slab(k), :]
            br_ref[slab(t), :] = (jax.nn.gelu(t0[slab(t), :]) * sg).astype(BF16)


def _mix_sample_call(l, x, mod, p, state_pool_tm, state_conv_tm):
    tm = M_SAMPLE

    def mod_spec(k):
        return pl.BlockSpec((None, DEC_BATCH, D_MODEL), lambda j: (l, 0, k))

    full3 = lambda shape: pl.BlockSpec((None,) + shape, lambda j: (l,) + (0,) * len(shape))
    cbi = lambda j: jnp.clip(j - 1, 0, 1)
    return pl.pallas_call(
        _mix_sample_kernel,
        grid=(4,),
        in_specs=[
            pl.BlockSpec((tm, D_MODEL), lambda j: (0, 0)),
            full3((1, D_MODEL)),
            mod_spec(0), mod_spec(1),
            pl.BlockSpec((None, D_MODEL, CB), lambda j: (_win_index_a(j), 0, 0)),
            pl.BlockSpec((None, D_MODEL, CB), lambda j: (_win_index_b(j), 0, 0)),
            pl.BlockSpec((None, D_MODEL, CB), lambda j: (_win_index_c(j), 0, 0)),
            full3((4, GROUP, GROUP)),
            full3((1, POOL_WIDTH)),
            pl.BlockSpec((None, CONV_K, CB), lambda j: (l, 0, cbi(j))),
            full3((1, SGU_WIDTH)),
            full3((DEC_SEQ * DEC_SEQ, SGU_WIDTH)),
            full3((DEC_SEQ, SGU_WIDTH)),
            full3((POOL_BUF * DEC_BATCH, POOL_WIDTH)),
            pl.BlockSpec((None, CONV_BUF * DEC_BATCH, CB), lambda j: (l, 0, cbi(j))),
        ],
        out_specs=[
            pl.BlockSpec((tm, CB), lambda j: (0, j)),
            pl.BlockSpec((POOL_BUF * DEC_BATCH, POOL_WIDTH), lambda j: (0, 0)),
            pl.BlockSpec((CONV_BUF * DEC_BATCH, CB), lambda j: (0, cbi(j))),
            pl.BlockSpec((tm, SGU_WIDTH), lambda j: (0, 0)),
        ],
        out_shape=[
            jax.ShapeDtypeStruct((tm, D_MODEL), BF16),
            jax.ShapeDtypeStruct((POOL_BUF * DEC_BATCH, POOL_WIDTH), F32),
            jax.ShapeDtypeStruct((CONV_BUF * DEC_BATCH, CONV_WIDTH), F32),
            jax.ShapeDtypeStruct((tm, SGU_WIDTH), F32),
        ],
        scratch_shapes=[
            pltpu.VMEM((tm, D_MODEL), BF16),
            pltpu.VMEM((tm, CB), F32),
            pltpu.VMEM((tm, CB), F32),
        ],
        compiler_params=pltpu.CompilerParams(
            dimension_semantics=("arbitrary",), vmem_limit_bytes=VMEM_LIMIT),
        name="mix_sample",
    )(x, p["norm1"], mod, mod, p["w_mix"], p["w_mix"], p["w_mix"], p["w_pool_grp"], p["pool_scale"],
      p["w_conv"], p["sgu_norm"], p["w_sgu_v"], p["b_sgu_v"], state_pool_tm, state_conv_tm)


def _blocked(n_blocks, width, col0=0):
    return [(b, 0, col0 + b * width, width) for b in range(n_blocks)]


_N_GATE_BLOCKS = D_MODEL // TN_GATE
_LAYOUTS = {
    "w_mix": ("w_in", OFF_GATE // CB, CB, _blocked(OFF_GATE // CB, CB)),
    "w_gate": ("w_in", _N_GATE_BLOCKS, 3 * TN_GATE,
               [(j, k * TN_GATE, OFF_GATE + k * D_MODEL + j * TN_GATE, TN_GATE)
                for j in range(_N_GATE_BLOCKS) for k in range(3)]),
    "w_br_pool": ("w_br_pool", _N_GATE_BLOCKS, TN_GATE, _blocked(_N_GATE_BLOCKS, TN_GATE)),
    "w_br_conv": ("w_br_conv", _N_GATE_BLOCKS, TN_GATE, _blocked(_N_GATE_BLOCKS, TN_GATE)),
    "w_br_sgu": ("w_br_sgu", _N_GATE_BLOCKS, TN_GATE, _blocked(_N_GATE_BLOCKS, TN_GATE)),
    "w_out": ("w_out", None, D_MODEL, [(None, 0, 0, D_MODEL)]),
    "w_ff1": ("w_ff1", D_FF // TF, TF, _blocked(D_FF // TF, TF)),
    "w_ff2": ("w_ff2", None, D_MODEL, [(None, 0, 0, D_MODEL)]),
}


def _first_layer_cast_kernel(*refs, cast_pieces):
    n = len(cast_pieces)
    _cast_slices(refs[:n], refs[n:], cast_pieces)


def _first_layer_cast_call(names, big):
    casts = [_cast_specs(0, name, big, FIRST_CAST_STEPS, lambda i, j: i) for name in names]
    outs = pl.pallas_call(
        functools.partial(_first_layer_cast_kernel, cast_pieces=tuple(c[4] for c in casts)),
        grid=(FIRST_CAST_STEPS, 1),
        in_specs=[c[0] for c in casts],
        out_specs=[c[1] for c in casts],
        out_shape=[c[2] for c in casts],
        compiler_params=pltpu.CompilerParams(
            dimension_semantics=("arbitrary", "arbitrary"), vmem_limit_bytes=VMEM_LIMIT),
        name="first_layer_cast",
    )(*[c[3] for c in casts])
    return dict(zip(names, outs))


def _cast_specs(l_next, name, big, n_steps, step_of):
    src, n_blocks, width, pieces = _LAYOUTS[name]
    _, rows, _ = big[src].shape
    src_cols = max(s0 + wd for (_, _, s0, wd) in pieces)
    rows_per = max(BF16_SUBLANES, rows // n_steps)
    reps = n_steps * rows_per // rows
    in_spec = pl.BlockSpec((None, rows_per, src_cols), lambda *idx: (l_next, step_of(*idx) // reps, 0))
    if n_blocks is None:
        out_spec = pl.BlockSpec((rows_per, width), lambda *idx: (step_of(*idx) // reps, 0))
        shape = (rows, width)
    else:
        out_spec = pl.BlockSpec((n_blocks, rows_per, width), lambda *idx: (0, step_of(*idx) // reps, 0))
        shape = (n_blocks, rows, width)
    return in_spec, out_spec, jax.ShapeDtypeStruct(shape, BF16), big[src], tuple(pieces)


def _cast_slices(src_refs, dst_refs, all_pieces):
    for src, dst, pieces in zip(src_refs, dst_refs, all_pieces):
        for b, d0, s0, wd in pieces:
            v = src[:, s0:s0 + wd].astype(BF16)
            if b is None:
                dst[:, d0:d0 + wd] = v
            else:
                dst[b, :, d0:d0 + wd] = v


def _gate_kernel(*refs, tm, n1, kb, cast_pieces):
    n_fixed = 11
    n_cast = len(cast_pieces)
    (x_ref, br_ref, n1_ref, sh_ref, sc_ref, g_ref, wg_ref,
     wbp_ref, wbc_ref, wbs_ref, wout_ref) = refs[:n_fixed]
    cast_src = refs[n_fixed:n_fixed + n_cast]
    o_ref = refs[n_fixed + n_cast]
    cast_dst = refs[n_fixed + 1 + n_cast:n_fixed + 1 + 2 * n_cast]
    hbuf = refs[n_fixed + 1 + 2 * n_cast]
    half = tm // 2
    slabs = half // ROWS

    def step(first, last):
        _cast_slices(cast_src, cast_dst, cast_pieces)
        for hh in range(2):
            rows = slice(hh * half, (hh + 1) * half)
            if first:
                for c in range(slabs):
                    _modulated_norm_slab(x_ref, n1_ref, sc_ref, sh_ref, hbuf,
                                         slice(hh * half + c * ROWS, hh * half + (c + 1) * ROWS))
            acc = None if first else o_ref[rows, :]
            for b in range(kb):
                gates = jax.nn.sigmoid(_dot(hbuf[rows, :], wg_ref[b]))
                m = gates[:, 0:TN_GATE] * _dot(br_ref[rows, 0:POOL_WIDTH], wbp_ref[b])
                m = m + (gates[:, TN_GATE:2 * TN_GATE]
                         * _dot(br_ref[rows, POOL_WIDTH:POOL_WIDTH + CONV_WIDTH], wbc_ref[b]))
                m = m + (gates[:, 2 * TN_GATE:]
                         * _dot(br_ref[rows, POOL_WIDTH + CONV_WIDTH:], wbs_ref[b]))
                d = _dot(m.astype(BF16), wout_ref[b * TN_GATE:(b + 1) * TN_GATE, :])
                acc = d if acc is None else acc + d
            if not last:
                o_ref[rows, :] = acc
                continue
            for c in range(slabs):
                sl = slice(hh * half + c * ROWS, hh * half + (c + 1) * ROWS)
                o_ref[sl, :] = x_ref[sl, :] + g_ref[...] * acc[c * ROWS:(c + 1) * ROWS, :]

    _first_middle_last(pl.program_id(1), n1, step)


def _gate_call(l, x, br, p, grp, cast_from):
    tm, n_tiles, mod_arr, mod_spec = grp["tm"], grp["n_tiles"], grp["mod"], grp["mod_spec"]
    kb = grp["blocks_per_step"]
    n1 = _N_GATE_BLOCKS // kb

    def branch_spec(width):
        return pl.BlockSpec((kb, width, TN_GATE), lambda i, j: (j, 0, 0))

    casts = []
    if cast_from is not None:
        casts = [_cast_specs(l + 1, name, cast_from, n_tiles * n1, lambda i, j: i * n1 + j)
                 for name in _GATE_CAST]
    kern = functools.partial(_gate_kernel, tm=tm, n1=n1, kb=kb,
                             cast_pieces=tuple(c[4] for c in casts))
    outs = pl.pallas_call(
        kern,
        grid=(n_tiles, n1),
        in_specs=[
            pl.BlockSpec((tm, D_MODEL), lambda i, j: (i, 0), pipeline_mode=pl.Buffered(1)),
            pl.BlockSpec((tm, D_MODEL), lambda i, j: (i, 0)),
            pl.BlockSpec((None, 1, D_MODEL), lambda i, j: (l, 0, 0)),
            mod_spec(l, 0), mod_spec(l, 1), mod_spec(l, 2),
            pl.BlockSpec((kb, D_MODEL, 3 * TN_GATE), lambda i, j: (j, 0, 0)),
            branch_spec(POOL_WIDTH), branch_spec(CONV_WIDTH), branch_spec(SGU_WIDTH),
            pl.BlockSpec((kb * TN_GATE, D_MODEL), lambda i, j: (j, 0)),
        ] + [c[0] for c in casts],
        out_specs=[pl.BlockSpec((tm, D_MODEL), lambda i, j: (i, 0))] + [c[1] for c in casts],
        out_shape=[jax.ShapeDtypeStruct(x.shape, F32)] + [c[2] for c in casts],
        scratch_shapes=[pltpu.VMEM((tm, D_MODEL), BF16)],
        compiler_params=pltpu.CompilerParams(
            dimension_semantics=("arbitrary", "arbitrary"), vmem_limit_bytes=VMEM_LIMIT),
        name="gate_" + grp["name"],
    )(x, br, p["norm1"], mod_arr, mod_arr, mod_arr, p["w_gate"],
      p["w_br_pool"], p["w_br_conv"], p["w_br_sgu"], p["w_out"], *[c[3] for c in casts])
    return outs[0], outs[1:]


def _ffn_kernel(*refs, tm, nf, kb, final_norm, cast_pieces):
    n_cast = len(cast_pieces)
    x_ref, n2_ref, sh_ref, sc_ref, g_ref, fn_ref, w1_ref, w2_ref = refs[:8]
    cast_src = refs[8:8 + n_cast]
    o_ref = refs[8 + n_cast]
    cast_dst = refs[9 + n_cast:9 + 2 * n_cast]
    hbuf = refs[9 + 2 * n_cast]
    half = tm // 2
    slabs = half // ROWS

    def step(first, last):
        _cast_slices(cast_src, cast_dst, cast_pieces)
        for hh in range(2):
            rows = slice(hh * half, (hh + 1) * half)
            if first:
                for c in range(slabs):
                    _modulated_norm_slab(x_ref, n2_ref, sc_ref, sh_ref, hbuf,
                                         slice(hh * half + c * ROWS, hh * half + (c + 1) * ROWS))
            acc = None if first else o_ref[rows, :]
            for b in range(kb):
                a = jnp.square(jax.nn.relu(_dot(hbuf[rows, :], w1_ref[b]))).astype(BF16)
                d = _dot(a, w2_ref[b * TF:(b + 1) * TF, :])
                acc = d if acc is None else acc + d
            if not last:
                o_ref[rows, :] = acc
                continue
            for c in range(slabs):
                sl = slice(hh * half + c * ROWS, hh * half + (c + 1) * ROWS)
                y = x_ref[sl, :] + g_ref[...] * acc[c * ROWS:(c + 1) * ROWS, :]
                if final_norm:
                    y = _rms_scale(y, fn_ref[...])
                o_ref[sl, :] = y

    _first_middle_last(pl.program_id(1), nf, step)


def _ffn_call(l, x, p, grp, final_norm, cast_from):
    tm, n_tiles, mod_arr, mod_spec = grp["tm"], grp["n_tiles"], grp["mod"], grp["mod_spec"]
    kb = grp["blocks_per_step"]
    nf = D_FF // TF // kb
    casts = []
    if cast_from is not None:
        casts = [_cast_specs(l + 1, name, cast_from, n_tiles * nf, lambda i, f: i * nf + f)
                 for name in _FFN_CAST]
    kern = functools.partial(_ffn_kernel, tm=tm, nf=nf, kb=kb, final_norm=final_norm,
                             cast_pieces=tuple(c[4] for c in casts))
    outs = pl.pallas_call(
        kern,
        grid=(n_tiles, nf),
        in_specs=[
            pl.BlockSpec((tm, D_MODEL), lambda i, f: (i, 0)),
            pl.BlockSpec((None, 1, D_MODEL), lambda i, f: (l, 0, 0)),
            mod_spec(l, 3), mod_spec(l, 4), mod_spec(l, 5),
            pl.BlockSpec((1, D_MODEL), lambda i, f: (0, 0)),
            pl.BlockSpec((kb, D_MODEL, TF), lambda i, f: (f, 0, 0)),
            pl.BlockSpec((kb * TF, D_MODEL), lambda i, f: (f, 0)),
        ] + [c[0] for c in casts],
        out_specs=[pl.BlockSpec((tm, D_MODEL), lambda i, f: (i, 0))] + [c[1] for c in casts],
        out_shape=[jax.ShapeDtypeStruct(x.shape, F32)] + [c[2] for c in casts],
        scratch_shapes=[pltpu.VMEM((tm, D_MODEL), BF16)],
        compiler_params=pltpu.CompilerParams(
            dimension_semantics=("arbitrary", "arbitrary"), vmem_limit_bytes=VMEM_LIMIT),
        name="ffn_" + grp["name"],
    )(x, p["norm2"], mod_arr, mod_arr, mod_arr, p["final_norm"], p["w_ff1"], p["w_ff2"],
      *[c[3] for c in casts])
    return outs[0], outs[1:]


_GATE_CAST = ("w_mix", "w_gate", "w_br_pool", "w_br_conv", "w_br_sgu", "w_out")
_FFN_CAST = ("w_ff1",)


def kernel(x_prompt, x_sample, state_pool, state_conv, c_prompt, c_sample, norm1, norm2, w_ada, b_ada,
           w_in, w_pool_grp, pool_scale, w_conv, sgu_norm, w_sgu, b_sgu, w_br_pool, w_br_conv,
           w_br_sgu, w_out, w_ff1, w_ff2, final_norm):
    T, nb = DEC_SEQ, DEC_BATCH
    tps = SEQ // TM_PROMPT

    c_all = jnp.concatenate(
        [c_sample, c_prompt, jnp.zeros((ADA_ROWS - nb - BATCH, D_MODEL), F32)], axis=0)
    mod = _ada_call(c_all, w_ada, b_ada)
    modp = mod[:, nb:nb + BATCH].reshape(DEPTH, BATCH, N_MOD, 1, D_MODEL)

    small = {
        "norm1": norm1.reshape(DEPTH, 1, D_MODEL),
        "norm2": norm2.reshape(DEPTH, 1, D_MODEL),
        "final_norm": final_norm.reshape(1, D_MODEL),
        "w_pool_grp": w_pool_grp,
        "pool_scale": pool_scale.reshape(DEPTH, 1, POOL_WIDTH),
        "w_conv": w_conv,
        "sgu_norm": sgu_norm.reshape(DEPTH, 1, SGU_WIDTH),
        "w_sgu": w_sgu,
        "b_sgu_t": jnp.transpose(b_sgu, (0, 2, 1)),
        "w_sgu_v": jnp.repeat(
            jnp.transpose(w_sgu[:, :, :T, :T], (0, 2, 3, 1)).reshape(DEPTH, T * T, 4), GROUP, axis=-1),
        "b_sgu_v": jnp.repeat(jnp.transpose(b_sgu[:, :, :T], (0, 2, 1)), GROUP, axis=-1),
    }
    big = {"w_in": w_in, "w_br_pool": w_br_pool, "w_br_conv": w_br_conv, "w_br_sgu": w_br_sgu,
           "w_out": w_out, "w_ff1": w_ff1, "w_ff2": w_ff2}
    wl = _first_layer_cast_call(_GATE_CAST + _FFN_CAST, big)

    grp_p = {
        "name": "prompt", "tm": TM_PROMPT, "n_tiles": M_PROMPT // TM_PROMPT, "mod": modp,
        "blocks_per_step": 1,
        "mod_spec": lambda l, k: pl.BlockSpec(
            (None, None, None, 1, D_MODEL), lambda i, j: (l, i // tps, k, 0, 0)),
    }
    grp_s = {
        "name": "sample", "tm": M_SAMPLE, "n_tiles": 1, "mod": mod,
        "blocks_per_step": 2,
        "mod_spec": lambda l, k: pl.BlockSpec((None, nb, D_MODEL), lambda i, j: (l, 0, k)),
    }

    xp = x_prompt.reshape(M_PROMPT, D_MODEL)
    xs = jnp.transpose(x_sample, (1, 0, 2)).reshape(M_SAMPLE, D_MODEL)
    spool_tm = jnp.transpose(state_pool, (0, 2, 1, 3)).reshape(DEPTH, POOL_BUF * nb, POOL_WIDTH)
    sconv_tm = jnp.transpose(state_conv, (0, 2, 1, 3)).reshape(DEPTH, CONV_BUF * nb, CONV_WIDTH)

    pool_p, conv_p, v_p, pool_s, conv_s, v_s = [], [], [], [], [], []
    last_tile = slice(SEQ // TM_MIX - 1, None, SEQ // TM_MIX)
    for l in range(DEPTH):
        final = l == DEPTH - 1
        p = dict(small, **wl)
        gate_cast = None if final else big
        ffn_cast = None if final else big

        br, ptail, ztail, vtail, p["w_ff2"] = _mix_prompt_call(l, xp, modp, p, big)
        xp, gate_next = _gate_call(l, xp, br, p, grp_p, gate_cast)
        xp, ffn_next = _ffn_call(l, xp, p, grp_p, final, ffn_cast)
        pool_p.append(ptail[last_tile, POOL_HALO - POOL_BUF:, :])
        conv_p.append(ztail[last_tile, CONV_HALO - CONV_BUF:, :])
        v_p.append(vtail.reshape(-1, CHUNK, SGU_WIDTH)[last_tile])

        br, npool, nconv, nv = _mix_sample_call(l, xs, mod, p, spool_tm, sconv_tm)
        xs, _ = _gate_call(l, xs, br, p, grp_s, None)
        xs, _ = _ffn_call(l, xs, p, grp_s, final, None)
        pool_s.append(jnp.transpose(npool.reshape(POOL_BUF, nb, POOL_WIDTH), (1, 0, 2)))
        conv_s.append(jnp.transpose(nconv.reshape(CONV_BUF, nb, CONV_WIDTH), (1, 0, 2)))
        v_s.append(jnp.transpose(nv.reshape(T, nb, SGU_WIDTH), (1, 0, 2)))
        if not final:
            wl = dict(zip(_GATE_CAST + _FFN_CAST, list(gate_next) + list(ffn_next)))

    y_prompt = xp.reshape(BATCH, SEQ, D_MODEL)
    y_sample = jnp.transpose(xs.reshape(T, nb, D_MODEL), (1, 0, 2))
    return (y_prompt, y_sample, jnp.stack(pool_p), jnp.stack(conv_p), jnp.stack(v_p),
            jnp.stack(pool_s), jnp.stack(conv_s), jnp.stack(v_s))
```

```python
import functools

import jax
import jax.numpy as jnp
from jax import lax
from jax.experimental import pallas as pl
from jax.experimental.pallas import tpu as pltpu

F32 = jnp.float32
BF16 = jnp.bfloat16

D_MODEL = 2048
DEPTH = 4
BATCH = 4
SEQ = 2048
DEC_BATCH = 128
DEC_SEQ = 4
PAST_LEN = 16384
POOL_WINDOWS = (2, 4, 8, 16)
GROUP = 128
POOL_WIDTH = 512
POOL_BUF = 15
CONV_WIDTH = 1024
CONV_K = 3
CONV_BUF = 2
CHUNK = 128
SGU_WIDTH = 512
N_MOD = 6
D_FF = 4 * D_MODEL
EPS = 1e-6
OFF_GATE = POOL_WIDTH + 3 * CONV_WIDTH + 2 * SGU_WIDTH
N_IN = OFF_GATE + 3 * D_MODEL

ROWS = 128
CB = 512
POOL_HALO = 16
CONV_HALO = 8
TM_PROMPT = 1024
TM_MIX = 512
M_PROMPT = BATCH * SEQ
M_SAMPLE = DEC_BATCH * DEC_SEQ
TN_GATE = 256
TF = 1024
BF16_SUBLANES = 16
FIRST_CAST_STEPS = 32
TN_ADA = 1024
ADA_ROWS = 136
VMEM_LIMIT = 60 * 1024 * 1024


def _dot(a, b):
    return jnp.dot(a, b, preferred_element_type=F32)


def _rms_scale(x, g):
    return x * lax.rsqrt(jnp.mean(x * x, axis=-1, keepdims=True) + EPS) * g


def _modulated_norm_slab(x_ref, g_ref, sc_ref, sh_ref, h_ref, rows):
    x = x_ref[rows, :]
    r = lax.rsqrt(jnp.mean(x * x, axis=-1, keepdims=True) + EPS)
    h = x_ref[rows, :] * r * (g_ref[...] * (1.0 + sc_ref[...])) + sh_ref[...]
    h_ref[rows, :] = h.astype(BF16)


def _modulated_norm(x_ref, g_ref, sc_ref, sh_ref, h_ref, tm):
    def body(c, carry):
        rows = pl.ds(pl.multiple_of(c * ROWS, ROWS), ROWS)
        _modulated_norm_slab(x_ref, g_ref, sc_ref, sh_ref, h_ref, rows)
        return carry
    lax.fori_loop(0, tm // ROWS, body, 0)


def _first_middle_last(step_index, n_steps, body):
    pl.when(step_index == 0)(lambda: body(True, False))
    pl.when(jnp.logical_and(step_index > 0, step_index < n_steps - 1))(lambda: body(False, False))
    pl.when(step_index == n_steps - 1)(lambda: body(False, True))


def _ada_kernel(c_ref, w_ref, b_ref, o_ref):
    a = jax.nn.silu(c_ref[...]).astype(BF16)
    o_ref[...] = _dot(a, w_ref[...].astype(BF16)) + b_ref[...]


def _ada_call(c_all, w_ada, b_ada):
    n = N_MOD * D_MODEL
    return pl.pallas_call(
        _ada_kernel,
        grid=(DEPTH, n // TN_ADA),
        in_specs=[
            pl.BlockSpec((ADA_ROWS, D_MODEL), lambda l, j: (0, 0)),
            pl.BlockSpec((None, D_MODEL, TN_ADA), lambda l, j: (l, 0, j)),
            pl.BlockSpec((None, 1, TN_ADA), lambda l, j: (l, 0, j)),
        ],
        out_specs=pl.BlockSpec((None, ADA_ROWS, TN_ADA), lambda l, j: (l, 0, j)),
        out_shape=jax.ShapeDtypeStruct((DEPTH, ADA_ROWS, n), F32),
        compiler_params=pltpu.CompilerParams(
            dimension_semantics=("arbitrary", "arbitrary"), vmem_limit_bytes=VMEM_LIMIT),
        name="ada",
    )(c_all, w_ada, b_ada.reshape(DEPTH, 1, n))


def _mix_prompt_kernel(x_ref, n1_ref, sh_ref, sc_ref, w_ref, wgrp_ref, pscale_ref,
                       wconv_ref, sgun_ref, wsgu_ref, bsgu_ref, cast_src_ref,
                       br_ref, ptail_ref, ztail_ref, vtail_ref, cast_dst_ref,
                       hbuf, ubuf, vbuf, pbuf, zbuf, bcbuf, *, tm, tiles_per_seq, cast_pieces):
    tile_in_seq = pl.program_id(0) % tiles_per_seq
    first_tile = tile_in_seq == 0
    n_slabs = tm // ROWS
    n_cb = CONV_WIDTH // CB
    slab = lambda c: slice(c * ROWS, (c + 1) * ROWS)
    proj = lambda b: _dot(hbuf[...], w_ref[b])

    @pl.when(pl.program_id(0) == 0)
    def _():
        pbuf[tm:tm + POOL_HALO, :] = jnp.zeros((POOL_HALO, CB), F32)
        zbuf[:, tm:tm + CONV_HALO, :] = jnp.zeros((n_cb, CONV_HALO, CB), F32)

    _modulated_norm(x_ref, n1_ref, sc_ref, sh_ref, hbuf, tm)

    _cast_slices([cast_src_ref], [cast_dst_ref], [cast_pieces])

    pbuf[0:POOL_HALO, :] = jnp.where(first_tile, 0.0, pbuf[tm:tm + POOL_HALO, :])
    for cb in range(n_cb):
        zbuf[cb, 0:CONV_HALO, :] = jnp.where(first_tile, 0.0, zbuf[cb, tm:tm + CONV_HALO, :])

    ubuf[...] = proj(7)
    vbuf[...] = proj(8)
    pbuf[POOL_HALO:POOL_HALO + tm, :] = proj(0)
    ptail_ref[...] = pbuf[tm:tm + POOL_HALO, :]

    row = lax.broadcasted_iota(jnp.int32, (CHUNK, CHUNK), 0)
    col = lax.broadcasted_iota(jnp.int32, (CHUNK, CHUNK), 1)
    n_grp = SGU_WIDTH // GROUP
    wt = [jnp.where(row >= col, wsgu_ref[g], 0.0).astype(BF16) for g in range(n_grp)]
    for c in range(n_slabs):
        vn = _rms_scale(jax.nn.gelu(vbuf[slab(c), :]), sgun_ref[...])
        if c == n_slabs - 1:
            vtail_ref[...] = vn
        for g in range(n_grp):
            cols = slice(g * GROUP, (g + 1) * GROUP)
            sg = _dot(wt[g], vn[:, cols].astype(BF16)) + bsgu_ref[:, g:g + 1]
            out = jax.nn.gelu(ubuf[slab(c), cols]) * sg
            br_ref[slab(c), POOL_WIDTH + CONV_WIDTH + g * GROUP:
                   POOL_WIDTH + CONV_WIDTH + (g + 1) * GROUP] = out.astype(BF16)

    def conv_proj(cb):
        zbuf[cb, CONV_HALO:CONV_HALO + tm, :] = proj(5 + cb) * proj(1 + cb)
        bcbuf[cb] = proj(3 + cb)
        ztail_ref[:, cb * CB:(cb + 1) * CB] = zbuf[cb, tm:tm + CONV_HALO, :]

    def conv_mixer(cb):
        wcols = slice(cb * CB, (cb + 1) * CB)
        for c in range(n_slabs):
            ext = zbuf[cb, c * ROWS:(c + 1) * ROWS + CONV_HALO, :]
            conv = (wconv_ref[0:1, wcols] * pltpu.roll(ext, 2, axis=0)[CONV_HALO:, :]
                    + wconv_ref[1:2, wcols] * pltpu.roll(ext, 1, axis=0)[CONV_HALO:, :]
                    + wconv_ref[2:3, wcols] * ext[CONV_HALO:, :])
            br_ref[slab(c), POOL_WIDTH + cb * CB:POOL_WIDTH + (cb + 1) * CB] = (
                bcbuf[cb, slab(c), :] * conv).astype(BF16)

    conv_proj(0)
    pos0 = tile_in_seq * tm
    for c in range(n_slabs):
        pos = pos0 + c * ROWS + lax.broadcasted_iota(jnp.int32, (ROWS, 1), 0)
        for g, w in enumerate(POOL_WINDOWS):
            cols = slice(g * GROUP, (g + 1) * GROUP)
            ext = pbuf[c * ROWS:(c + 1) * ROWS + POOL_HALO, cols]
            s = ext
            k = 1
            while k < w:
                s = s + pltpu.roll(s, k, axis=0)
                k *= 2
            cnt = jnp.minimum(pos + 1, w).astype(F32)
            d = s[POOL_HALO:, :] / cnt - ext[POOL_HALO:, :]
            y = _dot(d.astype(BF16), wgrp_ref[g].astype(BF16)) * pscale_ref[:, cols]
            br_ref[slab(c), cols] = y.astype(BF16)

    conv_proj(1)
    conv_mixer(0)
    conv_mixer(1)


def _win_index_a(j):
    return jnp.where(j == 3, 7, j)


def _win_index_b(j):
    return jnp.where(j == 3, 8, jnp.maximum(j, 1) + 2)


def _win_index_c(j):
    return jnp.clip(j, 1, 2) + 4


def _mix_prompt_call(l, x, modp, p, big):
    tm = TM_MIX
    tps = SEQ // tm
    n_tiles = M_PROMPT // tm
    n_cb = CONV_WIDTH // CB

    def mod_spec(k):
        return pl.BlockSpec((None, None, None, 1, D_MODEL), lambda i: (l, i // tps, k, 0, 0))

    full3 = lambda shape: pl.BlockSpec((None,) + shape, lambda i: (l,) + (0,) * len(shape))
    cast = _cast_specs(l, "w_ff2", big, n_tiles, lambda i: i)
    kern = functools.partial(_mix_prompt_kernel, tm=tm, tiles_per_seq=tps, cast_pieces=cast[4])
    return pl.pallas_call(
        kern,
        grid=(n_tiles,),
        in_specs=[
            pl.BlockSpec((tm, D_MODEL), lambda i: (i, 0)),
            full3((1, D_MODEL)),
            mod_spec(0), mod_spec(1),
            pl.BlockSpec((OFF_GATE // CB, D_MODEL, CB), lambda i: (0, 0, 0)),
            full3((4, GROUP, GROUP)),
            full3((1, POOL_WIDTH)),
            full3((CONV_K, CONV_WIDTH)),
            full3((1, SGU_WIDTH)),
            full3((4, CHUNK, CHUNK)),
            full3((CHUNK, 4)),
            cast[0],
        ],
        out_specs=[
            pl.BlockSpec((tm, D_MODEL), lambda i: (i, 0)),
            pl.BlockSpec((None, POOL_HALO, POOL_WIDTH), lambda i: (i, 0, 0)),
            pl.BlockSpec((None, CONV_HALO, CONV_WIDTH), lambda i: (i, 0, 0)),
            pl.BlockSpec((CHUNK, SGU_WIDTH), lambda i: (i, 0)),
            cast[1],
        ],
        out_shape=[
            jax.ShapeDtypeStruct((M_PROMPT, D_MODEL), BF16),
            jax.ShapeDtypeStruct((n_tiles, POOL_HALO, POOL_WIDTH), F32),
            jax.ShapeDtypeStruct((n_tiles, CONV_HALO, CONV_WIDTH), F32),
            jax.ShapeDtypeStruct((n_tiles * CHUNK, SGU_WIDTH), F32),
            cast[2],
        ],
        scratch_shapes=[
            pltpu.VMEM((tm, D_MODEL), BF16),
            pltpu.VMEM((tm, SGU_WIDTH), F32),
            pltpu.VMEM((tm, SGU_WIDTH), F32),
            pltpu.VMEM((tm + POOL_HALO, POOL_WIDTH), F32),
            pltpu.VMEM((n_cb, tm + CONV_HALO, CB), F32),
            pltpu.VMEM((n_cb, tm, CB), F32),
        ],
        compiler_params=pltpu.CompilerParams(
            dimension_semantics=("arbitrary",), vmem_limit_bytes=VMEM_LIMIT),
        name="mix_prompt",
    )(x, p["norm1"], modp, modp, p["w_mix"], p["w_pool_grp"], p["pool_scale"],
      p["w_conv"], p["sgu_norm"], p["w_sgu"], p["b_sgu_t"], cast[3])


def _mix_sample_kernel(x_ref, n1_ref, sh_ref, sc_ref, wa_ref, wb_ref, wc_ref, wgrp_ref, pscale_ref,
                       wconv_ref, sgun_ref, wv_ref, bv_ref, spool_ref, sconv_ref,
                       br_ref, pool_ref, conv_ref, v_ref,
                       hbuf, t0, t1):
    j = pl.program_id(0)
    nb = DEC_BATCH
    T = DEC_SEQ

    def slab(t):
        return slice(t * nb, (t + 1) * nb)

    @pl.when(j == 0)
    def _():
        _modulated_norm(x_ref, n1_ref, sc_ref, sh_ref, hbuf, T * nb)

    @pl.when(j == 0)
    def _pool():
        t0[...] = _dot(hbuf[...], wa_ref[...])
        keep = POOL_BUF - T
        pool_ref[0:keep * nb, :] = spool_ref[T * nb:POOL_BUF * nb, :]
        pool_ref[keep * nb:POOL_BUF * nb, :] = t0[...]

        def full(s, cols):
            if s < POOL_BUF:
                return spool_ref[slab(s), cols]
            return t0[slab(s - POOL_BUF), cols]

        for t in range(T):
            for g, w in enumerate(POOL_WINDOWS):
                cols = slice(g * GROUP, (g + 1) * GROUP)
                s = full(POOL_BUF + t, cols)
                for k in range(1, w):
                    s = s + full(POOL_BUF + t - k, cols)
                cnt = float(min(PAST_LEN + t + 1, w))
                d = s / cnt - t0[slab(t), cols]
                y = _dot(d.astype(BF16), wgrp_ref[g].astype(BF16)) * pscale_ref[:, cols]
                br_ref[slab(t), cols] = y.astype(BF16)

    @pl.when(jnp.logical_and(j >= 1, j <= 2))
    def _conv():
        h = hbuf[...]
        t1[...] = _dot(h, wc_ref[...]) * _dot(h, wa_ref[...])
        t0[...] = _dot(h, wb_ref[...])

        def full(s):
            if s < CONV_BUF:
                return sconv_ref[slab(s), :]
            return t1[slab(s - CONV_BUF), :]

        for s in range(CONV_BUF):
            conv_ref[slab(s), :] = full(T + s)
        for t in range(T):
            conv = (wconv_ref[0:1, :] * full(t) + wconv_ref[1:2, :] * full(t + 1)
                    + wconv_ref[2:3, :] * full(t + 2))
            br_ref[slab(t), :] = (t0[slab(t), :] * conv).astype(BF16)

    @pl.when(j == 3)
    def _sgu():
        h = hbuf[...]
        t0[...] = _dot(h, wa_ref[...])
        t1[...] = _dot(h, wb_ref[...])
        for t in range(T):
            v_ref[slab(t), :] = _rms_scale(jax.nn.gelu(t1[slab(t), :]), sgun_ref[...])
        for t in range(T):
            sg = bv_ref[t:t + 1, :]
            for k in range(t + 1):
                sg = sg + wv_ref[t * T + k:t * T + k + 1, :] * v_ref[slab(k), :]
            br_ref[slab(t), :] = (jax.nn.gelu(t0[slab(t), :]) * sg).astype(BF16)


def _mix_sample_call(l, x, mod, p, state_pool_tm, state_conv_tm):
    tm = M_SAMPLE

    def mod_spec(k):
        return pl.BlockSpec((None, DEC_BATCH, D_MODEL), lambda j: (l, 0, k))

    full3 = lambda shape: pl.BlockSpec((None,) + shape, lambda j: (l,) + (0,) * len(shape))
    cbi = lambda j: jnp.clip(j - 1, 0, 1)
    return pl.pallas_call(
        _mix_sample_kernel,
        grid=(4,),
        in_specs=[
            pl.BlockSpec((tm, D_MODEL), lambda j: (0, 0)),
            full3((1, D_MODEL)),
            mod_spec(0), mod_spec(1),
            pl.BlockSpec((None, D_MODEL, CB), lambda j: (_win_index_a(j), 0, 0)),
            pl.BlockSpec((None, D_MODEL, CB), lambda j: (_win_index_b(j), 0, 0)),
            pl.BlockSpec((None, D_MODEL, CB), lambda j: (_win_index_c(j), 0, 0)),
            full3((4, GROUP, GROUP)),
            full3((1, POOL_WIDTH)),
            pl.BlockSpec((None, CONV_K, CB), lambda j: (l, 0, cbi(j))),
            full3((1, SGU_WIDTH)),
            full3((DEC_SEQ * DEC_SEQ, SGU_WIDTH)),
            full3((DEC_SEQ, SGU_WIDTH)),
            full3((POOL_BUF * DEC_BATCH, POOL_WIDTH)),
            pl.BlockSpec((None, CONV_BUF * DEC_BATCH, CB), lambda j: (l, 0, cbi(j))),
        ],
        out_specs=[
            pl.BlockSpec((tm, CB), lambda j: (0, j)),
            pl.BlockSpec((POOL_BUF * DEC_BATCH, POOL_WIDTH), lambda j: (0, 0)),
            pl.BlockSpec((CONV_BUF * DEC_BATCH, CB), lambda j: (0, cbi(j))),
            pl.BlockSpec((tm, SGU_WIDTH), lambda j: (0, 0)),
        ],
        out_shape=[
            jax.ShapeDtypeStruct((tm, D_MODEL), BF16),
            jax.ShapeDtypeStruct((POOL_BUF * DEC_BATCH, POOL_WIDTH), F32),
            jax.ShapeDtypeStruct((CONV_BUF * DEC_BATCH, CONV_WIDTH), F32),
            jax.ShapeDtypeStruct((tm, SGU_WIDTH), F32),
        ],
        scratch_shapes=[
            pltpu.VMEM((tm, D_MODEL), BF16),
            pltpu.VMEM((tm, CB), F32),
            pltpu.VMEM((tm, CB), F32),
        ],
        compiler_params=pltpu.CompilerParams(
            dimension_semantics=("arbitrary",), vmem_limit_bytes=VMEM_LIMIT),
        name="mix_sample",
    )(x, p["norm1"], mod, mod, p["w_mix"], p["w_mix"], p["w_mix"], p["w_pool_grp"], p["pool_scale"],
      p["w_conv"], p["sgu_norm"], p["w_sgu_v"], p["b_sgu_v"], state_pool_tm, state_conv_tm)


def _blocked(n_blocks, width, col0=0):
    return [(b, 0, col0 + b * width, width) for b in range(n_blocks)]


_N_GATE_BLOCKS = D_MODEL // TN_GATE
_LAYOUTS = {
    "w_mix": ("w_in", OFF_GATE // CB, CB, _blocked(OFF_GATE // CB, CB)),
    "w_gate": ("w_in", _N_GATE_BLOCKS, 3 * TN_GATE,
               [(j, k * TN_GATE, OFF_GATE + k * D_MODEL + j * TN_GATE, TN_GATE)
                for j in range(_N_GATE_BLOCKS) for k in range(3)]),
    "w_br_pool": ("w_br_pool", _N_GATE_BLOCKS, TN_GATE, _blocked(_N_GATE_BLOCKS, TN_GATE)),
    "w_br_conv": ("w_br_conv", _N_GATE_BLOCKS, TN_GATE, _blocked(_N_GATE_BLOCKS, TN_GATE)),
    "w_br_sgu": ("w_br_sgu", _N_GATE_BLOCKS, TN_GATE, _blocked(_N_GATE_BLOCKS, TN_GATE)),
    "w_out": ("w_out", None, D_MODEL, [(None, 0, 0, D_MODEL)]),
    "w_ff1": ("w_ff1", D_FF // TF, TF, _blocked(D_FF // TF, TF)),
    "w_ff2": ("w_ff2", None, D_MODEL, [(None, 0, 0, D_MODEL)]),
}


def _first_layer_cast_kernel(*refs, cast_pieces):
    n = len(cast_pieces)
    _cast_slices(refs[:n], refs[n:], cast_pieces)


def _first_layer_cast_call(names, big):
    casts = [_cast_specs(0, name, big, FIRST_CAST_STEPS, lambda i, j: i) for name in names]
    outs = pl.pallas_call(
        functools.partial(_first_layer_cast_kernel, cast_pieces=tuple(c[4] for c in casts)),
        grid=(FIRST_CAST_STEPS, 1),
        in_specs=[c[0] for c in casts],
        out_specs=[c[1] for c in casts],
        out_shape=[c[2] for c in casts],
        compiler_params=pltpu.CompilerParams(
            dimension_semantics=("arbitrary", "arbitrary"), vmem_limit_bytes=VMEM_LIMIT),
        name="first_layer_cast",
    )(*[c[3] for c in casts])
    return dict(zip(names, outs))


def _cast_specs(l_next, name, big, n_steps, step_of):
    src, n_blocks, width, pieces = _LAYOUTS[name]
    _, rows, _ = big[src].shape
    src_cols = max(s0 + wd for (_, _, s0, wd) in pieces)
    rows_per = max(BF16_SUBLANES, rows // n_steps)
    reps = n_steps * rows_per // rows
    in_spec = pl.BlockSpec((None, rows_per, src_cols), lambda *idx: (l_next, step_of(*idx) // reps, 0))
    if n_blocks is None:
        out_spec = pl.BlockSpec((rows_per, width), lambda *idx: (step_of(*idx) // reps, 0))
        shape = (rows, width)
    else:
        out_spec = pl.BlockSpec((n_blocks, rows_per, width), lambda *idx: (0, step_of(*idx) // reps, 0))
        shape = (n_blocks, rows, width)
    return in_spec, out_spec, jax.ShapeDtypeStruct(shape, BF16), big[src], tuple(pieces)


def _cast_slices(src_refs, dst_refs, all_pieces):
    for src, dst, pieces in zip(src_refs, dst_refs, all_pieces):
        for b, d0, s0, wd in pieces:
            v = src[:, s0:s0 + wd].astype(BF16)
            if b is None:
                dst[:, d0:d0 + wd] = v
            else:
                dst[b, :, d0:d0 + wd] = v


def _gate_kernel(*refs, tm, n1, kb, cast_pieces):
    n_fixed = 11
    n_cast = len(cast_pieces)
    (x_ref, br_ref, n1_ref, sh_ref, sc_ref, g_ref, wg_ref,
     wbp_ref, wbc_ref, wbs_ref, wout_ref) = refs[:n_fixed]
    cast_src = refs[n_fixed:n_fixed + n_cast]
    o_ref = refs[n_fixed + n_cast]
    cast_dst = refs[n_fixed + 1 + n_cast:n_fixed + 1 + 2 * n_cast]
    hbuf = refs[n_fixed + 1 + 2 * n_cast]
    half = tm // 2
    slabs = half // ROWS

    def step(first, last):
        _cast_slices(cast_src, cast_dst, cast_pieces)
        for hh in range(2):
            rows = slice(hh * half, (hh + 1) * half)
            if first:
                for c in range(slabs):
                    _modulated_norm_slab(x_ref, n1_ref, sc_ref, sh_ref, hbuf,
                                         slice(hh * half + c * ROWS, hh * half + (c + 1) * ROWS))
            acc = None if first else o_ref[rows, :]
            for b in range(kb):
                gates = jax.nn.sigmoid(_dot(hbuf[rows, :], wg_ref[b]))
                m = gates[:, 0:TN_GATE] * _dot(br_ref[rows, 0:POOL_WIDTH], wbp_ref[b])
                m = m + (gates[:, TN_GATE:2 * TN_GATE]
                         * _dot(br_ref[rows, POOL_WIDTH:POOL_WIDTH + CONV_WIDTH], wbc_ref[b]))
                m = m + (gates[:, 2 * TN_GATE:]
                         * _dot(br_ref[rows, POOL_WIDTH + CONV_WIDTH:], wbs_ref[b]))
                d = _dot(m.astype(BF16), wout_ref[b * TN_GATE:(b + 1) * TN_GATE, :])
                acc = d if acc is None else acc + d
            if not last:
                o_ref[rows, :] = acc
                continue
            for c in range(slabs):
                sl = slice(hh * half + c * ROWS, hh * half + (c + 1) * ROWS)
                o_ref[sl, :] = x_ref[sl, :] + g_ref[...] * acc[c * ROWS:(c + 1) * ROWS, :]

    _first_middle_last(pl.program_id(1), n1, step)


def _gate_call(l, x, br, p, grp, cast_from):
    tm, n_tiles, mod_arr, mod_spec = grp["tm"], grp["n_tiles"], grp["mod"], grp["mod_spec"]
    kb = grp["blocks_per_step"]
    n1 = _N_GATE_BLOCKS // kb

    def branch_spec(width):
        return pl.BlockSpec((kb, width, TN_GATE), lambda i, j: (j, 0, 0))

    casts = []
    if cast_from is not None:
        casts = [_cast_specs(l + 1, name, cast_from, n_tiles * n1, lambda i, j: i * n1 + j)
                 for name in _GATE_CAST]
    kern = functools.partial(_gate_kernel, tm=tm, n1=n1, kb=kb,
                             cast_pieces=tuple(c[4] for c in casts))
    outs = pl.pallas_call(
        kern,
        grid=(n_tiles, n1),
        in_specs=[
            pl.BlockSpec((tm, D_MODEL), lambda i, j: (i, 0), pipeline_mode=pl.Buffered(1)),
            pl.BlockSpec((tm, D_MODEL), lambda i, j: (i, 0)),
            pl.BlockSpec((None, 1, D_MODEL), lambda i, j: (l, 0, 0)),
            mod_spec(l, 0), mod_spec(l, 1), mod_spec(l, 2),
            pl.BlockSpec((kb, D_MODEL, 3 * TN_GATE), lambda i, j: (j, 0, 0)),
            branch_spec(POOL_WIDTH), branch_spec(CONV_WIDTH), branch_spec(SGU_WIDTH),
            pl.BlockSpec((kb * TN_GATE, D_MODEL), lambda i, j: (j, 0)),
        ] + [c[0] for c in casts],
        out_specs=[pl.BlockSpec((tm, D_MODEL), lambda i, j: (i, 0))] + [c[1] for c in casts],
        out_shape=[jax.ShapeDtypeStruct(x.shape, F32)] + [c[2] for c in casts],
        scratch_shapes=[pltpu.VMEM((tm, D_MODEL), BF16)],
        compiler_params=pltpu.CompilerParams(
            dimension_semantics=("arbitrary", "arbitrary"), vmem_limit_bytes=VMEM_LIMIT),
        name="gate_" + grp["name"],
    )(x, br, p["norm1"], mod_arr, mod_arr, mod_arr, p["w_gate"],
      p["w_br_pool"], p["w_br_conv"], p["w_br_sgu"], p["w_out"], *[c[3] for c in casts])
    return outs[0], outs[1:]


def _ffn_kernel(*refs, tm, nf, kb, final_norm, cast_pieces):
    n_cast = len(cast_pieces)
    x_ref, n2_ref, sh_ref, sc_ref, g_ref, fn_ref, w1_ref, w2_ref = refs[:8]
    cast_src = refs[8:8 + n_cast]
    o_ref = refs[8 + n_cast]
    cast_dst = refs[9 + n_cast:9 + 2 * n_cast]
    hbuf = refs[9 + 2 * n_cast]
    half = tm // 2
    slabs = half // ROWS

    def step(first, last):
        _cast_slices(cast_src, cast_dst, cast_pieces)
        for hh in range(2):
            rows = slice(hh * half, (hh + 1) * half)
            if first:
                for c in range(slabs):
                    _modulated_norm_slab(x_ref, n2_ref, sc_ref, sh_ref, hbuf,
                                         slice(hh * half + c * ROWS, hh * half + (c + 1) * ROWS))
            acc = None if first else o_ref[rows, :]
            for b in range(kb):
                a = jnp.square(jax.nn.relu(_dot(hbuf[rows, :], w1_ref[b]))).astype(BF16)
                d = _dot(a, w2_ref[b * TF:(b + 1) * TF, :])
                acc = d if acc is None else acc + d
            if not last:
                o_ref[rows, :] = acc
                continue
            for c in range(slabs):
                sl = slice(hh * half + c * ROWS, hh * half + (c + 1) * ROWS)
                y = x_ref[sl, :] + g_ref[...] * acc[c * ROWS:(c + 1) * ROWS, :]
                if final_norm:
                    y = _rms_scale(y, fn_ref[...])
                o_ref[sl, :] = y

    _first_middle_last(pl.program_id(1), nf, step)


def _ffn_call(l, x, p, grp, final_norm, cast_from):
    tm, n_tiles, mod_arr, mod_spec = grp["tm"], grp["n_tiles"], grp["mod"], grp["mod_spec"]
    kb = grp["blocks_per_step"]
    nf = D_FF // TF // kb
    casts = []
    if cast_from is not None:
        casts = [_cast_specs(l + 1, name, cast_from, n_tiles * nf, lambda i, f: i * nf + f)
                 for name in _FFN_CAST]
    kern = functools.partial(_ffn_kernel, tm=tm, nf=nf, kb=kb, final_norm=final_norm,
                             cast_pieces=tuple(c[4] for c in casts))
    outs = pl.pallas_call(
        kern,
        grid=(n_tiles, nf),
        in_specs=[
            pl.BlockSpec((tm, D_MODEL), lambda i, f: (i, 0)),
            pl.BlockSpec((None, 1, D_MODEL), lambda i, f: (l, 0, 0)),
            mod_spec(l, 3), mod_spec(l, 4), mod_spec(l, 5),
            pl.BlockSpec((1, D_MODEL), lambda i, f: (0, 0)),
            pl.BlockSpec((kb, D_MODEL, TF), lambda i, f: (f, 0, 0)),
            pl.BlockSpec((kb * TF, D_MODEL), lambda i, f: (f, 0)),
        ] + [c[0] for c in casts],
        out_specs=[pl.BlockSpec((tm, D_MODEL), lambda i, f: (i, 0))] + [c[1] for c in casts],
        out_shape=[jax.ShapeDtypeStruct(x.shape, F32)] + [c[2] for c in casts],
        scratch_shapes=[pltpu.VMEM((tm, D_MODEL), BF16)],
        compiler_params=pltpu.CompilerParams(
            dimension_semantics=("arbitrary", "arbitrary"), vmem_limit_bytes=VMEM_LIMIT),
        name="ffn_" + grp["name"],
    )(x, p["norm2"], mod_arr, mod_arr, mod_arr, p["final_norm"], p["w_ff1"], p["w_ff2"],
      *[c[3] for c in casts])
    return outs[0], outs[1:]


_GATE_CAST = ("w_mix", "w_gate", "w_br_pool", "w_br_conv", "w_br_sgu", "w_out")
_FFN_CAST = ("w_ff1",)


def kernel(x_prompt, x_sample, state_pool, state_conv, c_prompt, c_sample, norm1, norm2, w_ada, b_ada,
           w_in, w_pool_grp, pool_scale, w_conv, sgu_norm, w_sgu, b_sgu, w_br_pool, w_br_conv,
           w_br_sgu, w_out, w_ff1, w_ff2, final_norm):
    T, nb = DEC_SEQ, DEC_BATCH
    tps = SEQ // TM_PROMPT

    c_all = jnp.concatenate(
        [c_sample, c_prompt, jnp.zeros((ADA_ROWS - nb - BATCH, D_MODEL), F32)], axis=0)
    mod = _ada_call(c_all, w_ada, b_ada)
    modp = mod[:, nb:nb + BATCH].reshape(DEPTH, BATCH, N_MOD, 1, D_MODEL)

    small = {
        "norm1": norm1.reshape(DEPTH, 1, D_MODEL),
        "norm2": norm2.reshape(DEPTH, 1, D_MODEL),
        "final_norm": final_norm.reshape(1, D_MODEL),
        "w_pool_grp": w_pool_grp,
        "pool_scale": pool_scale.reshape(DEPTH, 1, POOL_WIDTH),
        "w_conv": w_conv,
        "sgu_norm": sgu_norm.reshape(DEPTH, 1, SGU_WIDTH),
        "w_sgu": w_sgu,
        "b_sgu_t": jnp.transpose(b_sgu, (0, 2, 1)),
        "w_sgu_v": jnp.repeat(
            jnp.transpose(w_sgu[:, :, :T, :T], (0, 2, 3, 1)).reshape(DEPTH, T * T, 4), GROUP, axis=-1),
        "b_sgu_v": jnp.repeat(jnp.transpose(b_sgu[:, :, :T], (0, 2, 1)), GROUP, axis=-1),
    }
    big = {"w_in": w_in, "w_br_pool": w_br_pool, "w_br_conv": w_br_conv, "w_br_sgu": w_br_sgu,
           "w_out": w_out, "w_ff1": w_ff1, "w_ff2": w_ff2}
    wl = _first_layer_cast_call(_GATE_CAST + _FFN_CAST, big)

    grp_p = {
        "name": "prompt", "tm": TM_PROMPT, "n_tiles": M_PROMPT // TM_PROMPT, "mod": modp,
        "blocks_per_step": 1,
        "mod_spec": lambda l, k: pl.BlockSpec(
            (None, None, None, 1, D_MODEL), lambda i, j: (l, i // tps, k, 0, 0)),
    }
    grp_s = {
        "name": "sample", "tm": M_SAMPLE, "n_tiles": 1, "mod": mod,
        "blocks_per_step": 1,
        "mod_spec": lambda l, k: pl.BlockSpec((None, nb, D_MODEL), lambda i, j: (l, 0, k)),
    }

    xp = x_prompt.reshape(M_PROMPT, D_MODEL)
    xs = jnp.transpose(x_sample, (1, 0, 2)).reshape(M_SAMPLE, D_MODEL)
    spool_tm = jnp.transpose(state_pool, (0, 2, 1, 3)).reshape(DEPTH, POOL_BUF * nb, POOL_WIDTH)
    sconv_tm = jnp.transpose(state_conv, (0, 2, 1, 3)).reshape(DEPTH, CONV_BUF * nb, CONV_WIDTH)

    pool_p, conv_p, v_p, pool_s, conv_s, v_s = [], [], [], [], [], []
    last_tile = slice(SEQ // TM_MIX - 1, None, SEQ // TM_MIX)
    for l in range(DEPTH):
        final = l == DEPTH - 1
        p = dict(small, **wl)
        gate_cast = None if final else big
        ffn_cast = None if final else big

        br, ptail, ztail, vtail, p["w_ff2"] = _mix_prompt_call(l, xp, modp, p, big)
        xp, gate_next = _gate_call(l, xp, br, p, grp_p, gate_cast)
        xp, ffn_next = _ffn_call(l, xp, p, grp_p, final, ffn_cast)
        pool_p.append(ptail[last_tile, POOL_HALO - POOL_BUF:, :])
        conv_p.append(ztail[last_tile, CONV_HALO - CONV_BUF:, :])
        v_p.append(vtail.reshape(-1, CHUNK, SGU_WIDTH)[last_tile])

        br, npool, nconv, nv = _mix_sample_call(l, xs, mod, p, spool_tm, sconv_tm)
        xs, _ = _gate_call(l, xs, br, p, grp_s, None)
        xs, _ = _ffn_call(l, xs, p, grp_s, final, None)
        pool_s.append(jnp.transpose(npool.reshape(POOL_BUF, nb, POOL_WIDTH), (1, 0, 2)))
        conv_s.append(jnp.transpose(nconv.reshape(CONV_BUF, nb, CONV_WIDTH), (1, 0, 2)))
        v_s.append(jnp.transpose(nv.reshape(T, nb, SGU_WIDTH), (1, 0, 2)))
        if not final:
            wl = dict(zip(_GATE_CAST + _FFN_CAST, list(gate_next) + list(ffn_next)))

    y_prompt = xp.reshape(BATCH, SEQ, D_MODEL)
    y_sample = jnp.transpose(xs.reshape(T, nb, D_MODEL), (1, 0, 2))
    return (y_prompt, y_sample, jnp.stack(pool_p), jnp.stack(conv_p), jnp.stack(v_p),
            jnp.stack(pool_s), jnp.stack(conv_s), jnp.stack(v_s))
```
